```python
import math, functools
import jax, jax.numpy as jnp
from jax import lax
import numpy as np

D_MODEL = 1024
BATCH = 2
SEQ = 8192
DEPTH = 4
DEC_BATCH = 32
DEC_SEQ = 8
PAST_LEN = 8192
PAGE_SIZE = 128

HEAD_DIM = 64
A_HEADS = 6
A_WIDTH = A_HEADS * HEAD_DIM
CONV_W = 4
DELTA_CHUNK = 64
POOL_WINDOWS = (2, 4, 8, 16)
POOL_GROUP = 64
B_WIDTH = POOL_GROUP * len(POOL_WINDOWS)
POOL_STATE = max(POOL_WINDOWS) - 1
C_HEADS = 6
C_WIDTH = C_HEADS * HEAD_DIM
IDX_HEADS = 4
IDX_DIM = 64
TOPK_MAX = 256
QBLK = 128
ROT_DIM = HEAD_DIM // 4
ROPE_THETA = 500000.0
MIX_WIDTH = A_WIDTH + B_WIDTH + C_WIDTH
D_FF = -(-8 * D_MODEL // (3 * 256)) * 256
EPS = 1e-6
IN_SPLITS = (3 * A_WIDTH, A_WIDTH, A_HEADS, A_HEADS,
             B_WIDTH,
             C_WIDTH, C_WIDTH, C_WIDTH,
             IDX_HEADS * IDX_DIM, IDX_DIM, IDX_HEADS)
IN_COLS = sum(IN_SPLITS)

kernel_name = 'hybrid_delta_pool_dsa_step'


def rms_norm(x, g):
    xf = x.astype(jnp.float32)
    y = xf * lax.rsqrt(jnp.mean(xf * xf, -1, keepdims=True) + EPS)
    return (y * g.astype(jnp.float32)).astype(x.dtype)


def split_cols(p):
    out, o = [], 0
    for n in IN_SPLITS:
        out.append(p[..., o:o + n])
        o += n
    return out


def rope_tables(pos):
    half = ROT_DIM // 2
    inv = ROPE_THETA ** (-jnp.arange(half, dtype=jnp.float32) / half)
    ang = pos.astype(jnp.float32)[:, None] * inv[None, :]
    return jnp.cos(ang), jnp.sin(ang)


def apply_rope(x, cos, sin):
    half = ROT_DIM // 2
    extra = x.ndim - 3
    c = cos.reshape((cos.shape[0],) + (1,) * extra + (half,)).astype(x.dtype)
    s = sin.reshape((sin.shape[0],) + (1,) * extra + (half,)).astype(x.dtype)
    x1, x2 = x[..., :half], x[..., half:ROT_DIM]
    return jnp.concatenate([x1 * c - x2 * s, x2 * c + x1 * s, x[..., ROT_DIM:]], -1)


def causal_conv(x, prev, w):
    T = x.shape[1]
    xx = jnp.concatenate([prev.astype(x.dtype), x], 1)
    y = xx[:, 0:T] * w[0]
    for j in range(1, CONV_W):
        y = y + xx[:, j:j + T] * w[j]
    return jax.nn.silu(y), xx[:, -(CONV_W - 1):]


def gated_delta(q, k, v, g, beta, h0):
    B, T, H, D = q.shape
    C = math.gcd(T, DELTA_CHUNK)
    N = T // C
    q = q * lax.rsqrt(jnp.sum(q * q, -1, keepdims=True) + 1e-6) * (D ** -0.5)
    k = k * lax.rsqrt(jnp.sum(k * k, -1, keepdims=True) + 1e-6)

    def chunks(a):
        a = a.reshape((B, N, C, H) + a.shape[3:])
        return jnp.moveaxis(a, (1, 3), (0, 2))

    qc, kc, vc, bc = chunks(q), chunks(k), chunks(v), chunks(beta)
    gc = jnp.cumsum(chunks(g), -1)
    incl = jnp.tril(jnp.ones((C, C), bool))
    strict = jnp.tril(jnp.ones((C, C), bool), -1)
    diff = gc[..., :, None] - gc[..., None, :]
    decay = jnp.where(incl, jnp.exp(jnp.where(incl, diff, 0.0)), 0.0)
    kb = kc * bc[..., None]
    vb = vc * bc[..., None]
    lmat = jnp.where(strict, jnp.einsum('nbhcd,nbhsd->nbhcs', kb, kc) * decay, 0.0)
    amat = lmat + jnp.eye(C, dtype=lmat.dtype)
    u = lax.linalg.triangular_solve(amat, vb, left_side=True, lower=True, unit_diagonal=True)
    w = lax.linalg.triangular_solve(amat, kb * jnp.exp(gc)[..., None], left_side=True, lower=True,
                                    unit_diagonal=True)
    intra = jnp.where(incl, jnp.einsum('nbhcd,nbhsd->nbhcs', qc, kc) * decay, 0.0)

    def step(h, xs):
        qi, ki, ui, wi, gi, ai = xs
        v_new = ui - jnp.einsum('bhcd,bhde->bhce', wi, h)
        o = (jnp.einsum('bhcd,bhde->bhce', qi * jnp.exp(gi)[..., None], h)
             + jnp.einsum('bhcs,bhse->bhce', ai, v_new))
        glast = gi[..., -1]
        h = (h * jnp.exp(glast)[..., None, None]
             + jnp.einsum('bhcd,bhce->bhde', ki * jnp.exp(glast[..., None] - gi)[..., None], v_new))
        return h, o

    h_T, o = lax.scan(step, h0, (qc, kc, u, w, gc, intra))
    o = jnp.moveaxis(o, (0, 2), (1, 3)).reshape(B, T, H, D)
    return o, h_T


def pool_mixer(u, prev, pos, pool_w, pool_scale):
    f32 = jnp.float32
    B, T, Cn = u.shape
    uf = u.astype(f32)
    xx = jnp.concatenate([prev.astype(f32), uf], 1)
    cs = jnp.concatenate([jnp.zeros((B, 1, Cn), f32), jnp.cumsum(xx, 1)], 1)
    outs = []
    for gi, win in enumerate(POOL_WINDOWS):
        sl = slice(gi * POOL_GROUP, (gi + 1) * POOL_GROUP)
        hi = cs[:, POOL_STATE + 1:POOL_STATE + 1 + T, sl]
        lo = cs[:, POOL_STATE + 1 - win:POOL_STATE + 1 - win + T, sl]
        cnt = jnp.minimum(win, pos + 1).astype(f32)
        mean = (hi - lo) / cnt[None, :, None]
        outs.append(jnp.einsum('btc,cd->btd', mean - uf[..., sl], pool_w[gi].astype(f32)))
    y = jnp.concatenate(outs, -1) * pool_scale.astype(f32)
    return y.astype(u.dtype), xx[:, -POOL_STATE:].astype(u.dtype)


def index_and_attend(q, qi, wi, qpos, ki_all, gather_kv, k_sel):
    f32 = jnp.float32
    L = ki_all.shape[1]
    dots = jnp.einsum('bqhd,bsd->bqhs', qi.astype(f32), ki_all.astype(f32)) * (IDX_DIM ** -0.5)
    score = jnp.einsum('bqhs,bqh->bqs', jax.nn.relu(dots), wi.astype(f32) * (IDX_HEADS ** -0.5))
    adm = jnp.arange(L)[None, :] <= qpos[:, None]
    score = jnp.where(adm[None], score, -jnp.inf)
    _, idx = lax.top_k(score, k_sel)
    valid = idx <= qpos[None, :, None]
    kg, vg = gather_kv(idx)
    logits = jnp.einsum('bqhd,bqkhd->bqhk', q.astype(f32), kg.astype(f32)) * (HEAD_DIM ** -0.5)
    logits = jnp.where(valid[:, :, None, :], logits, -jnp.inf)
    p = jax.nn.softmax(logits, -1)
    return jnp.einsum('bqhk,bqkhd->bqhd', p, vg.astype(f32)).astype(q.dtype)


def take_rows(a, i):
    return jax.vmap(lambda ab, ib: ab[ib])(a, i)


def dsa_prompt(q, k, v, qi, ki, wi, pos):
    B, S = q.shape[:2]
    k_sel = min(TOPK_MAX, S // 4)

    def gather(idx):
        return take_rows(k, idx), take_rows(v, idx)

    def block(t0):
        sl = lambda a: lax.dynamic_slice_in_dim(a, t0, QBLK, 1)
        qpos = lax.dynamic_slice_in_dim(pos, t0, QBLK, 0)
        return index_and_attend(sl(q), sl(qi), sl(wi), qpos, ki, gather, k_sel)

    out = lax.map(block, jnp.arange(S // QBLK) * QBLK)
    return jnp.moveaxis(out, 0, 1).reshape(B, S, C_HEADS, HEAD_DIM)


def dsa_sample(pool_k, pool_v, pool_ki, page_table, q, k, v, qi, ki, wi, pos):
    B, T = q.shape[:2]
    past_len = page_table.shape[1] * PAGE_SIZE
    k_sel = min(TOPK_MAX, (past_len + T) // 4)
    ki_past = pool_ki[page_table].reshape(B, past_len, IDX_DIM).astype(ki.dtype)
    ki_all = jnp.concatenate([ki_past, ki], 1)

    def gather(idx):
        past = idx < past_len
        ip = jnp.where(past, idx, 0)
        phys = jnp.take_along_axis(page_table, (ip // PAGE_SIZE).reshape(B, -1), axis=1).reshape(idx.shape)
        off = ip % PAGE_SIZE
        inew = jnp.clip(idx - past_len, 0, T - 1)
        m = past[..., None, None]
        kg = jnp.where(m, pool_k[phys, off].astype(k.dtype), take_rows(k, inew))
        vg = jnp.where(m, pool_v[phys, off].astype(v.dtype), take_rows(v, inew))
        return kg, vg

    return index_and_attend(q, qi, wi, pos, ki_all, gather, k_sel)


def trunk_layer(x, pos, conv_prev, pool_prev, delta_prev, attend_c,
                norm1, w_in, conv_w, a_log, dt_bias, a_norm, pool_w, pool_scale,
                w_out, norm2, w_gate, w_up, w_down):
    f32 = jnp.float32
    B, T, _ = x.shape
    hn = rms_norm(x, norm1)
    (qkv_a, z_a, b_a, a_a, u_b, q_c, k_c, v_c, qi_c, ki_c, wi_c) = split_cols(hn @ w_in)
    conv_out, conv_new = causal_conv(qkv_a, conv_prev, conv_w)
    heads = lambda t: t.reshape(B, T, A_HEADS, HEAD_DIM).astype(f32)
    q_a = heads(conv_out[..., :A_WIDTH])
    k_a = heads(conv_out[..., A_WIDTH:2 * A_WIDTH])
    v_a = heads(conv_out[..., 2 * A_WIDTH:])
    beta = jax.nn.sigmoid(b_a.astype(f32))
    g = -jnp.exp(a_log.astype(f32)) * jax.nn.softplus(a_a.astype(f32) + dt_bias.astype(f32))
    o_a, delta_new = gated_delta(q_a, k_a, v_a, g, beta, delta_prev.astype(f32))
    o_a = (rms_norm(o_a, a_norm) * jax.nn.silu(heads(z_a))).reshape(B, T, A_WIDTH).astype(x.dtype)
    o_b, pool_new = pool_mixer(u_b, pool_prev, pos, pool_w, pool_scale)
    cos, sin = rope_tables(pos)
    q_c = apply_rope(q_c.reshape(B, T, C_HEADS, HEAD_DIM), cos, sin)
    k_c = apply_rope(k_c.reshape(B, T, C_HEADS, HEAD_DIM), cos, sin)
    v_c = v_c.reshape(B, T, C_HEADS, HEAD_DIM)
    qi_c = apply_rope(qi_c.reshape(B, T, IDX_HEADS, IDX_DIM), cos, sin)
    ki_c = apply_rope(ki_c, cos, sin)
    o_c = attend_c(q_c, k_c, v_c, qi_c, ki_c, wi_c, pos).reshape(B, T, C_WIDTH)
    x = x + jnp.concatenate([o_a, o_b, o_c], -1) @ w_out
    hn = rms_norm(x, norm2)
    x = x + (jax.nn.silu(hn @ w_gate) * (hn @ w_up)) @ w_down
    return x, (k_c, v_c, ki_c, conv_new, pool_new, delta_new.astype(x.dtype))


def setup_inputs(seed: int = 0) -> dict:
    key = jax.random.key(seed)
    ks = jax.random.split(key, 32)
    f32 = jnp.float32
    n_pages = PAST_LEN // PAGE_SIZE
    used = DEC_BATCH * n_pages
    n_pool = used + max(1, used // 4)
    nrm = lambda k, shape, s=1.0: jax.random.normal(k, shape, f32) * s
    perm = jax.random.permutation(ks[0], n_pool)
    page_table = perm[:used].reshape(DEC_BATCH, n_pages).astype(jnp.int32)
    dt = jax.random.uniform(ks[1], (DEPTH, A_HEADS), f32, 1e-3, 0.1)
    return {
        'x_prompt': nrm(ks[2], (BATCH, SEQ, D_MODEL)),
        'x_sample': nrm(ks[3], (DEC_BATCH, DEC_SEQ, D_MODEL)),
        'cache_k': nrm(ks[4], (DEPTH, n_pool, PAGE_SIZE, C_HEADS, HEAD_DIM)),
        'cache_v': nrm(ks[5], (DEPTH, n_pool, PAGE_SIZE, C_HEADS, HEAD_DIM)),
        'cache_kidx': nrm(ks[6], (DEPTH, n_pool, PAGE_SIZE, IDX_DIM)),
        'state_conv': nrm(ks[7], (DEPTH, DEC_BATCH, CONV_W - 1, 3 * A_WIDTH)),
        'state_pool': nrm(ks[8], (DEPTH, DEC_BATCH, POOL_STATE, B_WIDTH)),
        'state_delta': nrm(ks[9], (DEPTH, DEC_BATCH, A_HEADS, HEAD_DIM, HEAD_DIM), 0.1),
        'page_table': page_table,
        'norm1': 1.0 + nrm(ks[10], (DEPTH, D_MODEL), 0.05),
        'w_in': nrm(ks[11], (DEPTH, D_MODEL, IN_COLS), D_MODEL ** -0.5),
        'conv_w': nrm(ks[12], (DEPTH, CONV_W, 3 * A_WIDTH), CONV_W ** -0.5),
        'a_log': jnp.log(jax.random.uniform(ks[13], (DEPTH, A_HEADS), f32, 1.0, 16.0)),
        'dt_bias': jnp.log(jnp.expm1(dt)),
        'a_norm': 1.0 + nrm(ks[14], (DEPTH, HEAD_DIM), 0.05),
        'pool_w': nrm(ks[15], (DEPTH, len(POOL_WINDOWS), POOL_GROUP, POOL_GROUP), POOL_GROUP ** -0.5),
        'pool_scale': 1.0 + nrm(ks[16], (DEPTH, B_WIDTH), 0.05),
        'w_out': nrm(ks[17], (DEPTH, MIX_WIDTH, D_MODEL), MIX_WIDTH ** -0.5),
        'norm2': 1.0 + nrm(ks[18], (DEPTH, D_MODEL), 0.05),
        'w_gate': nrm(ks[19], (DEPTH, D_MODEL, D_FF), D_MODEL ** -0.5),
        'w_up': nrm(ks[20], (DEPTH, D_MODEL, D_FF), D_MODEL ** -0.5),
        'w_down': nrm(ks[21], (DEPTH, D_FF, D_MODEL), D_FF ** -0.5),
        'norm_f': 1.0 + nrm(ks[22], (D_MODEL,), 0.05),
    }


def reference(x_prompt, x_sample, cache_k, cache_v, cache_kidx, state_conv, state_pool, state_delta,
              page_table, norm1, w_in, conv_w, a_log, dt_bias, a_norm, pool_w, pool_scale,
              w_out, norm2, w_gate, w_up, w_down, norm_f):
    bp, tp = x_prompt.shape[:2]
    ts = x_sample.shape[1]
    past_len = page_table.shape[1] * PAGE_SIZE
    pos_p = jnp.arange(tp, dtype=jnp.int32)
    pos_s = past_len + jnp.arange(ts, dtype=jnp.int32)
    dt = x_prompt.dtype
    conv0 = jnp.zeros((bp, CONV_W - 1, 3 * A_WIDTH), dt)
    pool0 = jnp.zeros((bp, POOL_STATE, B_WIDTH), dt)
    delta0 = jnp.zeros((bp, A_HEADS, HEAD_DIM, HEAD_DIM), jnp.float32)
    hp, hs = x_prompt, x_sample
    p_st, s_st = [], []
    for l in range(DEPTH):
        wts = (norm1[l], w_in[l], conv_w[l], a_log[l], dt_bias[l], a_norm[l], pool_w[l], pool_scale[l],
               w_out[l], norm2[l], w_gate[l], w_up[l], w_down[l])
        hp, sp = trunk_layer(hp, pos_p, conv0, pool0, delta0, dsa_prompt, *wts)
        attend_s = functools.partial(dsa_sample, cache_k[l], cache_v[l], cache_kidx[l], page_table)
        hs, ss = trunk_layer(hs, pos_s, state_conv[l], state_pool[l], state_delta[l], attend_s, *wts)
        p_st.append(sp)
        s_st.append(ss)
    y_prompt = rms_norm(hp, norm_f)
    y_sample = rms_norm(hs, norm_f)
    p_k = jnp.stack([s[0] for s in p_st])
    p_v = jnp.stack([s[1] for s in p_st])
    p_kidx = jnp.stack([s[2] for s in p_st])
    p_conv = jnp.stack([s[3] for s in p_st])
    p_pool = jnp.stack([s[4] for s in p_st])
    p_delta = jnp.stack([s[5] for s in p_st])
    s_k = jnp.stack([s[0] for s in s_st])
    s_v = jnp.stack([s[1] for s in s_st])
    s_kidx = jnp.stack([s[2] for s in s_st])
    s_conv = jnp.stack([s[3] for s in s_st])
    s_pool = jnp.stack([s[4] for s in s_st])
    s_delta = jnp.stack([s[5] for s in s_st])
    return (y_prompt, y_sample, p_k, p_v, p_kidx, p_conv, p_pool, p_delta,
            s_k, s_v, s_kidx, s_conv, s_pool, s_delta)
```

```python
import functools
import math

import jax
import jax.numpy as jnp
import numpy as np
from jax import lax
from jax.experimental import pallas as pl
from jax.experimental.pallas import tpu as pltpu

F32, BF16, I32 = jnp.float32, jnp.bfloat16, jnp.int32

D_MODEL = 1024
HEAD_DIM = 64
A_HEADS = 6
A_WIDTH = A_HEADS * HEAD_DIM
CONV_W = 4
DELTA_CHUNK = 64
POOL_WINDOWS = (2, 4, 8, 16)
POOL_GROUP = 64
B_WIDTH = POOL_GROUP * len(POOL_WINDOWS)
POOL_STATE = max(POOL_WINDOWS) - 1
C_HEADS = 6
C_WIDTH = C_HEADS * HEAD_DIM
IDX_HEADS = 4
IDX_DIM = 64
QI_WIDTH = IDX_HEADS * IDX_DIM
TOPK_MAX = 256
ROT_DIM = HEAD_DIM // 4
ROT_HALF = ROT_DIM // 2
ROPE_THETA = 500000.0
D_FF = -(-8 * D_MODEL // (3 * 256)) * 256
EPS = 1e-6
PAGE_SIZE = 128

LANES_V7X = 128
SUBLANES_V7X = 8
VMEM_LIMIT_V7X = 56 * 1024 * 1024

MISC_W = LANES_V7X
MISC_KI, MISC_WI, MISC_B, MISC_A = 0, IDX_DIM, IDX_DIM + IDX_HEADS, IDX_DIM + IDX_HEADS + A_HEADS
OFF_QKV = 0
OFF_Z = OFF_QKV + 3 * A_WIDTH
OFF_U = OFF_Z + A_WIDTH
OFF_Q = OFF_U + B_WIDTH
OFF_K = OFF_Q + C_WIDTH
OFF_V = OFF_K + C_WIDTH
OFF_QI = OFF_V + C_WIDTH
OFF_MISC = OFF_QI + QI_WIDTH
IN_COLS_PAD = OFF_MISC + MISC_W

INT_MIN = -(2 ** 31)
NEG_INF_KEY = -2139095041
MASKED = -1e30
NT_DIMS = (((1,), (1,)), ((), ()))
BIG_IDX = 2 ** 30

_HI = lax.Precision.HIGHEST


def _hdot(a, b):
    return jnp.dot(a, b, precision=_HI, preferred_element_type=F32)


def _sigmoid(x):
    return 1.0 / (1.0 + jnp.exp(-x))


def _softplus(x):
    return jnp.maximum(x, 0.0) + jnp.log1p(jnp.exp(-jnp.abs(x)))


def _score_key(s):
    s = jnp.where(s == 0.0, 0.0, s)
    bits = lax.bitcast_convert_type(s, I32)
    return bits ^ ((bits >> 31) & 0x7FFFFFFF)


def _cut_rank(key, m, kidx):
    return jnp.where(key == m, kidx, jnp.where(key > m, -1, BIG_IDX))


def _params(*sem):
    return pltpu.CompilerParams(dimension_semantics=sem, vmem_limit_bytes=VMEM_LIMIT_V7X)


def _rope(y, c, sa, sb):
    w = y.shape[-1]
    return y * c + pltpu.roll(y, w - ROT_HALF, 1) * sa + pltpu.roll(y, ROT_HALF, 1) * sb


def _in_proj_kernel(x_ref, g_ref, w_ref, ct_ref, sa_ref, sb_ref,
                    qkv_ref, z_ref, u_ref, q_ref, k_ref, v_ref, qi_ref, misc_ref):
    x = x_ref[...]
    ms = jnp.mean(x * x, axis=-1, keepdims=True)
    hn = (x * lax.rsqrt(ms + EPS) * g_ref[...]).astype(BF16)

    def proj(a, b):
        return jnp.dot(hn, w_ref[:, a:b], preferred_element_type=F32)

    qkv_ref[...] = proj(OFF_QKV, OFF_Z)
    z_ref[...] = proj(OFF_Z, OFF_U)
    u_ref[...] = proj(OFF_U, OFF_Q)
    v_ref[...] = proj(OFF_V, OFF_QI)
    ct, sa, sb = ct_ref[...], sa_ref[...], sb_ref[...]

    def tiled(t, n):
        return jnp.concatenate([t] * n, axis=1)

    n3 = C_WIDTH // LANES_V7X
    q_ref[...] = _rope(proj(OFF_Q, OFF_K), tiled(ct, n3), tiled(sa, n3), tiled(sb, n3))
    k_ref[...] = _rope(proj(OFF_K, OFF_V), tiled(ct, n3), tiled(sa, n3), tiled(sb, n3))
    n2 = QI_WIDTH // LANES_V7X
    qi_ref[...] = _rope(proj(OFF_QI, OFF_MISC), tiled(ct, n2), tiled(sa, n2), tiled(sb, n2))
    is_ki = lax.broadcasted_iota(I32, ct.shape, 1) < IDX_DIM
    misc_ref[...] = _rope(proj(OFF_MISC, IN_COLS_PAD), jnp.where(is_ki, ct, 1.0),
                          jnp.where(is_ki, sa, 0.0), jnp.where(is_ki, sb, 0.0))


def _in_proj(x2d, g, w, tabs, seq_len, tm):
    n = x2d.shape[0]
    assert n % tm == 0
    if seq_len % tm == 0:
        per = seq_len // tm
        tab_map = lambda i: (i % per, 0)
    else:
        assert tm % seq_len == 0 and n == tm
        tabs = tuple(jnp.tile(t, (tm // seq_len, 1)) for t in tabs)
        tab_map = lambda i: (0, 0)
    row = lambda width: pl.BlockSpec((tm, width), lambda i: (i, 0))
    const = lambda shape: pl.BlockSpec(shape, lambda i: (0, 0))
    widths = (3 * A_WIDTH, A_WIDTH, B_WIDTH, C_WIDTH, C_WIDTH, C_WIDTH, QI_WIDTH, MISC_W)
    return pl.pallas_call(
        _in_proj_kernel,
        out_shape=tuple(jax.ShapeDtypeStruct((n, wd), F32) for wd in widths),
        grid=(n // tm,),
        in_specs=[row(D_MODEL), const((1, D_MODEL)), const((D_MODEL, IN_COLS_PAD))]
        + [pl.BlockSpec((tm, LANES_V7X), tab_map)] * 3,
        out_specs=tuple(row(wd) for wd in widths),
        compiler_params=_params("arbitrary"),
        name="in_proj",
    )(x2d, g, w, *tabs)


FFN_CHUNK = 256


def _out_ffn_kernel(x_ref, oa_ref, ob_ref, oc_ref, wo_ref, g2_ref, wg_ref, wu_ref, wd_ref, gf_ref,
                    y_ref, *, final):
    def mm(a, b):
        return jnp.dot(a, b, preferred_element_type=F32)

    mix = (mm(oa_ref[...].astype(BF16), wo_ref[0:A_WIDTH, :])
           + mm(ob_ref[...].astype(BF16), wo_ref[A_WIDTH:A_WIDTH + B_WIDTH, :])
           + mm(oc_ref[...].astype(BF16), wo_ref[A_WIDTH + B_WIDTH:, :]))
    x1 = x_ref[...] + mix
    ms = jnp.mean(x1 * x1, axis=-1, keepdims=True)
    hn = (x1 * lax.rsqrt(ms + EPS) * g2_ref[...]).astype(BF16)
    acc = x1
    for c0 in range(0, D_FF, FFN_CHUNK):
        gate = mm(hn, wg_ref[:, c0:c0 + FFN_CHUNK])
        up = mm(hn, wu_ref[:, c0:c0 + FFN_CHUNK])
        hid = (gate * _sigmoid(gate) * up).astype(BF16)
        acc = acc + mm(hid, wd_ref[c0:c0 + FFN_CHUNK, :])
    if final:
        ms2 = jnp.mean(acc * acc, axis=-1, keepdims=True)
        acc = acc * lax.rsqrt(ms2 + EPS) * gf_ref[...]
    y_ref[...] = acc


def _out_ffn(x2d, oa, ob, oc, wo, g2, wg, wu, wd, gf, tm, final):
    n = x2d.shape[0]
    assert n % tm == 0 and D_FF % FFN_CHUNK == 0
    row = lambda width: pl.BlockSpec((tm, width), lambda i: (i, 0))
    const = lambda shape: pl.BlockSpec(shape, lambda i: (0, 0))
    return pl.pallas_call(
        functools.partial(_out_ffn_kernel, final=final),
        out_shape=jax.ShapeDtypeStruct((n, D_MODEL), F32),
        grid=(n // tm,),
        in_specs=[row(D_MODEL), row(A_WIDTH), row(B_WIDTH), row(C_WIDTH),
                  const((D_MODEL, D_MODEL)), const((1, D_MODEL)),
                  const((D_MODEL, D_FF)), const((D_MODEL, D_FF)), const((D_FF, D_MODEL)),
                  const((1, D_MODEL))],
        out_specs=row(D_MODEL),
        compiler_params=_params("arbitrary"),
        name="out_ffn",
    )(x2d, oa, ob, oc, wo, g2, wg, wu, wd, gf)


POOL_HALO = max(POOL_WINDOWS)


def _pool_kernel(u_ref, prev_ref, w_ref, sc_ref, o_ref, xb_ref, *, pos0):
    t = pl.program_id(1)
    tt = u_ref.shape[1]

    @pl.when(t == 0)
    def _():
        xb_ref[0:POOL_HALO, :] = prev_ref[0]

    u = u_ref[0]
    xb_ref[POOL_HALO:POOL_HALO + tt, :] = u
    sums = {}
    s = u
    for j in range(1, POOL_HALO):
        s = s + xb_ref[POOL_HALO - j:POOL_HALO - j + tt, :]
        if j + 1 in POOL_WINDOWS:
            sums[j + 1] = s
    halo = xb_ref[tt:tt + POOL_HALO, :]
    xb_ref[0:POOL_HALO, :] = halo
    pos = pos0 + t * tt + lax.broadcasted_iota(I32, (tt, B_WIDTH), 0)
    group = lax.broadcasted_iota(I32, (tt, B_WIDTH), 1) // POOL_GROUP
    mean = jnp.zeros((tt, B_WIDTH), F32)
    for gi, win in enumerate(POOL_WINDOWS):
        cnt = jnp.minimum(win, pos + 1).astype(F32)
        mean = jnp.where(group == gi, sums[win] / cnt, mean)
    d = (mean - u).astype(BF16)
    o_ref[0] = jnp.dot(d, w_ref[...], preferred_element_type=F32) * sc_ref[...]


def _pool(u, prev16, wbd, scale, pos0, tt):
    b, t, _ = u.shape
    assert t % tt == 0
    return pl.pallas_call(
        functools.partial(_pool_kernel, pos0=pos0),
        out_shape=jax.ShapeDtypeStruct((b, t, B_WIDTH), F32),
        grid=(b, t // tt),
        in_specs=[pl.BlockSpec((1, tt, B_WIDTH), lambda i, j: (i, j, 0)),
                  pl.BlockSpec((1, POOL_HALO, B_WIDTH), lambda i, j: (i, 0, 0)),
                  pl.BlockSpec((B_WIDTH, B_WIDTH), lambda i, j: (0, 0)),
                  pl.BlockSpec((1, B_WIDTH), lambda i, j: (0, 0))],
        out_specs=pl.BlockSpec((1, tt, B_WIDTH), lambda i, j: (i, j, 0)),
        scratch_shapes=[pltpu.VMEM((tt + POOL_HALO, B_WIDTH), F32)],
        compiler_params=_params("arbitrary", "arbitrary"),
        name="pool",
    )(u, prev16, wbd, scale)


CONV_HALO = SUBLANES_V7X


def _delta_kernel(qkv_ref, z_ref, misc_ref, cprev_ref, h0_ref, cw_ref, alog_ref, dtb_ref, an_ref,
                  seg_ref, selb_ref, selg_ref, o_ref, hout_ref, cb_ref, hs_ref, *, chunk):
    t = pl.program_id(1)
    nt = pl.num_programs(1)
    tc = qkv_ref.shape[1]
    c = chunk
    cast = (lambda a: a.astype(BF16)) if c % 16 == 0 else (lambda a: a)

    def mm(a, b):
        return jnp.dot(cast(a), cast(b), preferred_element_type=F32)

    def mm_nt(a, b):
        return lax.dot_general(cast(a), cast(b), NT_DIMS, preferred_element_type=F32)

    @pl.when(t == 0)
    def _():
        cb_ref[0:CONV_HALO, :] = cprev_ref[0]
        hs_ref[...] = h0_ref[0]

    cb_ref[CONV_HALO:CONV_HALO + tc, :] = qkv_ref[0]
    base = CONV_HALO - (CONV_W - 1)
    y = cb_ref[base:base + tc, :] * cw_ref[0:1, :]
    for j in range(1, CONV_W):
        y = y + cb_ref[base + j:base + j + tc, :] * cw_ref[j:j + 1, :]
    tail = cb_ref[tc:tc + CONV_HALO, :]
    cb_ref[0:CONV_HALO, :] = tail
    y = y * _sigmoid(y)
    q, k, v = y[:, 0:A_WIDTH], y[:, A_WIDTH:2 * A_WIDTH], y[:, 2 * A_WIDTH:]
    seg = seg_ref[...]
    qn = q * lax.rsqrt(_hdot(q * q, seg) + 1e-6) * (HEAD_DIM ** -0.5)
    kn = k * lax.rsqrt(_hdot(k * k, seg) + 1e-6)
    m = misc_ref[0]
    beta_f = _hdot(_sigmoid(m), selb_ref[...])
    g_f = _hdot(-jnp.exp(alog_ref[...]) * _softplus(m + dtb_ref[...]), selg_ref[...])
    z = z_ref[0]

    ri = lax.broadcasted_iota(I32, (c, c), 0)
    ci = lax.broadcasted_iota(I32, (c, c), 1)
    incl = ri >= ci
    strict = ri > ci
    eye = (ri == ci).astype(F32)
    ltri = incl.astype(F32)
    ones_cc = jnp.ones((c, c), F32)
    r64 = lax.broadcasted_iota(I32, (HEAD_DIM, HEAD_DIM), 0)
    c64 = lax.broadcasted_iota(I32, (HEAD_DIM, HEAD_DIM), 1)
    eye64 = (r64 == c64).astype(F32)
    levels = max(int(math.log2(c)) - 1, 0)

    for ck in range(tc // c):
        rows = slice(ck * c, (ck + 1) * c)
        gc = _hdot(ltri, g_f[rows])
        eg = jnp.exp(gc)
        glast = gc[c - 1:c, :]
        beta = beta_f[rows]
        kb = kn[rows] * beta
        vb = v[rows] * beta
        kbg = kb * eg
        qg = qn[rows] * eg
        kdec = kn[rows] * jnp.exp(glast - gc)
        hdec = jnp.exp(glast)
        outs = []
        for h in range(A_HEADS):
            sl = slice(h * HEAD_DIM, (h + 1) * HEAD_DIM)
            gcol = gc[:, h * HEAD_DIM:h * HEAD_DIM + c]
            grow = _hdot(ones_cc, gcol * eye)
            decay = jnp.where(incl, jnp.exp(jnp.where(incl, gcol - grow, 0.0)), 0.0)
            lmat = jnp.where(strict, mm_nt(kb[:, sl], kn[rows, sl]) * decay, 0.0)
            tinv = eye - lmat
            pw = lmat
            for _ in range(levels):
                pw = _hdot(pw, pw)
                tinv = tinv + _hdot(tinv, pw)
            u = _hdot(tinv, vb[:, sl])
            w = _hdot(tinv, kbg[:, sl])
            intra = jnp.where(incl, mm_nt(qn[rows, sl], kn[rows, sl]) * decay, 0.0)
            hst = hs_ref[h]
            v_new = u - mm(w, hst)
            o = mm(qg[:, sl], hst) + mm(intra, v_new)
            kdec_t = mm_nt(eye64, kdec[:, sl])
            hs_ref[h] = hst * hdec[:, sl] + mm(kdec_t, v_new)
            ms = jnp.mean(o * o, axis=-1, keepdims=True)
            zh = z[rows, sl]
            outs.append(o * lax.rsqrt(ms + EPS) * an_ref[...] * (zh * _sigmoid(zh)))
        o_ref[0, rows, :] = jnp.concatenate(outs, axis=1)

    @pl.when(t == nt - 1)
    def _():
        hout_ref[0] = hs_ref[...]


def _delta(qkv, z, misc, cprev8, h0, cw8, alog, dtb, an, consts, tc, chunk):
    b, t, _ = qkv.shape
    assert t % tc == 0 and tc % chunk == 0
    seg, selb, selg = consts
    bt = lambda width: pl.BlockSpec((1, tc, width), lambda i, j: (i, j, 0))
    const = lambda shape: pl.BlockSpec(shape, lambda i, j: tuple(0 for _ in shape))
    hspec = pl.BlockSpec((1, A_HEADS, HEAD_DIM, HEAD_DIM), lambda i, j: (i, 0, 0, 0))
    return pl.pallas_call(
        functools.partial(_delta_kernel, chunk=chunk),
        out_shape=(jax.ShapeDtypeStruct((b, t, A_WIDTH), F32),
                   jax.ShapeDtypeStruct((b, A_HEADS, HEAD_DIM, HEAD_DIM), F32)),
        grid=(b, t // tc),
        in_specs=[bt(3 * A_WIDTH), bt(A_WIDTH), bt(MISC_W),
                  pl.BlockSpec((1, CONV_HALO, 3 * A_WIDTH), lambda i, j: (i, 0, 0)),
                  hspec, const((CONV_HALO, 3 * A_WIDTH)), const((1, MISC_W)), const((1, MISC_W)),
                  const((1, HEAD_DIM)), const((A_WIDTH, A_WIDTH)), const((MISC_W, A_WIDTH)),
                  const((MISC_W, A_WIDTH))],
        out_specs=(bt(A_WIDTH), hspec),
        scratch_shapes=[pltpu.VMEM((tc + CONV_HALO, 3 * A_WIDTH), F32),
                        pltpu.VMEM((A_HEADS, HEAD_DIM, HEAD_DIM), F32)],
        compiler_params=_params("arbitrary", "arbitrary"),
        name="delta",
    )(qkv, z, misc, cprev8, h0, cw8, alog, dtb, an, seg, selb, selg)


def _dsa_prompt_kernel(q_ref, qi_ref, miscq_ref, kit_ref, kt_ref, vt_ref, o_ref, skey_ref, jc_ref,
                       *, k_sel, idx_bits):
    i = pl.program_id(1)
    tq = q_ref.shape[1]
    tk = tq
    nkt = i + 1

    wt = miscq_ref[0].T[MISC_WI:MISC_WI + IDX_HEADS, :] * (IDX_DIM ** -0.5 * IDX_HEADS ** -0.5)
    qi = qi_ref[0].astype(BF16)
    qpos = i * tq + lax.broadcasted_iota(I32, (tk, tq), 1)
    krow = lax.broadcasted_iota(I32, (tk, tq), 0)

    def score_tile(j, carry):
        kit = kit_ref[0, j][:, MISC_KI:MISC_KI + IDX_DIM].astype(BF16)
        s = jnp.zeros((tk, tq), F32)
        for h in range(IDX_HEADS):
            d = lax.dot_general(kit, qi[:, h * IDX_DIM:(h + 1) * IDX_DIM], NT_DIMS,
                                preferred_element_type=F32)
            s = s + jnp.maximum(d, 0.0) * wt[h:h + 1, :]
        skey_ref[j] = jnp.where(j * tk + krow <= qpos, _score_key(s), NEG_INF_KEY)
        return carry

    lax.fori_loop(0, nkt, score_tile, 0)

    grp = tk // SUBLANES_V7X

    def count(pred):
        def body(j, acc):
            kt3 = skey_ref[j].reshape(grp, SUBLANES_V7X, tq)
            idx3 = (j * tk + krow).reshape(grp, SUBLANES_V7X, tq)
            return acc + jnp.sum(pred(kt3, idx3).astype(I32), axis=0)
        acc = lax.fori_loop(0, nkt, body, jnp.zeros((SUBLANES_V7X, tq), I32))
        return jnp.sum(acc, axis=0, keepdims=True)

    def count_ge(cand):
        c8 = jnp.broadcast_to(cand, (SUBLANES_V7X, tq))[None]
        return count(lambda kt3, idx3: kt3 >= c8)

    m = jnp.full((1, tq), INT_MIN, I32)
    for bit in range(31, -1, -1):
        cand = (m ^ INT_MIN) if bit == 31 else (m | (1 << bit))
        m = jnp.where(count_ge(cand) >= k_sel, cand, m)
    n_gt = count_ge(m + 1)
    n_ge = count_ge(m)
    need = k_sel - n_gt

    jc_ref[...] = jnp.full(jc_ref.shape, (1 << idx_bits) - 1, I32)

    @pl.when(jnp.max(n_ge) > k_sel)
    def _():
        m8 = jnp.broadcast_to(m, (SUBLANES_V7X, tq))[None]
        jcut = jnp.zeros((1, tq), I32)
        for bit in range(idx_bits - 1, -1, -1):
            cand = jcut | (1 << bit)
            c8 = jnp.broadcast_to(cand, (SUBLANES_V7X, tq))[None]
            below = count(lambda kt3, idx3: jnp.where(kt3 == m8, idx3, BIG_IDX) < c8)
            jcut = jnp.where(below < need, cand, jcut)
        jc_ref[...] = jnp.broadcast_to(jcut, jc_ref.shape)

    jcut = jc_ref[0:1, :]

    q = (q_ref[0] * (HEAD_DIM ** -0.5)).astype(BF16)

    def att_tile(j, carry):
        ms, ls, accs = carry
        key = skey_ref[j]
        kidx = j * tk + krow
        sel = jnp.where(kidx <= qpos, _cut_rank(key, m, kidx), BIG_IDX) <= jcut
        kt = kt_ref[0, j]
        vt = vt_ref[0, j]
        ms2, ls2, accs2 = [], [], []
        for h in range(C_HEADS):
            sl = slice(h * HEAD_DIM, (h + 1) * HEAD_DIM)
            s = lax.dot_general(kt[:, sl], q[:, sl], NT_DIMS, preferred_element_type=F32)
            s = jnp.where(sel, s, MASKED)
            mn = jnp.maximum(ms[h], jnp.max(s, axis=0, keepdims=True))
            alpha = jnp.exp(ms[h] - mn)
            p = jnp.exp(s - mn)
            ls2.append(alpha * ls[h] + jnp.sum(p, axis=0, keepdims=True))
            accs2.append(alpha * accs[h] + jnp.dot(vt[sl, :], p.astype(BF16), preferred_element_type=F32))
            ms2.append(mn)
        return tuple(ms2), tuple(ls2), tuple(accs2)

    init = (tuple(jnp.full((1, tq), MASKED, F32) for _ in range(C_HEADS)),
            tuple(jnp.zeros((1, tq), F32) for _ in range(C_HEADS)),
            tuple(jnp.zeros((HEAD_DIM, tq), F32) for _ in range(C_HEADS)))
    ms, ls, accs = lax.fori_loop(0, nkt, att_tile, init)
    o_t = jnp.concatenate([accs[h] / ls[h] for h in range(C_HEADS)], axis=0)
    o_ref[0] = o_t.T


def _dsa_prompt(q, k, v, qi, misc, tq):
    b, t, _ = q.shape
    assert t % tq == 0 and tq == LANES_V7X
    nt = t // tq
    k_sel = min(TOPK_MAX, t // 4)
    idx_bits = max(1, (t - 1).bit_length())
    kt = k.reshape(b, nt, tq, C_WIDTH).astype(BF16)
    vt = jnp.swapaxes(v.reshape(b, nt, tq, C_WIDTH), 2, 3).astype(BF16)
    kit = misc.reshape(b, nt, tq, MISC_W)
    qblk = lambda width: pl.BlockSpec((1, tq, width), lambda i, j: (i, j, 0))
    seq = lambda d2, d3: pl.BlockSpec((1, nt, d2, d3), lambda i, j: (i, 0, 0, 0))
    return pl.pallas_call(
        functools.partial(_dsa_prompt_kernel, k_sel=k_sel, idx_bits=idx_bits),
        out_shape=jax.ShapeDtypeStruct((b, t, C_WIDTH), F32),
        grid=(b, nt),
        in_specs=[qblk(C_WIDTH), qblk(QI_WIDTH), qblk(MISC_W),
                  seq(tq, MISC_W), seq(tq, C_WIDTH), seq(C_WIDTH, tq)],
        out_specs=qblk(C_WIDTH),
        scratch_shapes=[pltpu.VMEM((nt, tq, tq), I32), pltpu.VMEM((SUBLANES_V7X, tq), I32)],
        compiler_params=_params("arbitrary", "arbitrary"),
        name="dsa_prompt",
    )(q, qi, misc, kit, kt, vt)


def _dsa_sample_index_kernel(pt_ref, kidx_ref, qi_ref, misc_ref, skey_ref, m_ref, jc_ref, sk_ref,
                             *, k_sel, n_pages, idx_bits):
    j = pl.program_id(1)
    t = qi_ref.shape[1]
    misc = misc_ref[0]
    qi = qi_ref[0]
    qs = jnp.concatenate([qi[:, h * IDX_DIM:(h + 1) * IDX_DIM] for h in range(IDX_HEADS)], axis=0)
    ws = jnp.concatenate([misc[:, MISC_WI + h:MISC_WI + h + 1] for h in range(IDX_HEADS)], axis=0)
    ws = ws * (IDX_DIM ** -0.5 * IDX_HEADS ** -0.5)
    lane = lax.broadcasted_iota(I32, (t, PAGE_SIZE), 1)
    row = lax.broadcasted_iota(I32, (t, PAGE_SIZE), 0)

    def emit(ki_tile, adm):
        d = lax.dot_general(qs.astype(BF16), ki_tile.astype(BF16), NT_DIMS, preferred_element_type=F32)
        sw = jnp.maximum(d, 0.0) * ws
        s = sw[0:t]
        for h in range(1, IDX_HEADS):
            s = s + sw[h * t:(h + 1) * t]
        key = jnp.where(adm, _score_key(s), NEG_INF_KEY)
        sk_ref[j] = key
        skey_ref[0] = key

    @pl.when(j < n_pages)
    def _():
        emit(kidx_ref[0, 0], lane >= 0)

    @pl.when(j == n_pages)
    def _():
        ki_new = misc[:, MISC_KI:MISC_KI + IDX_DIM]
        ki_pad = jnp.concatenate([ki_new, jnp.zeros((PAGE_SIZE - t, IDX_DIM), F32)], axis=0)
        emit(ki_pad, lane <= row)
        allk = sk_ref[...]
        idx = (lax.broadcasted_iota(I32, allk.shape, 0) * PAGE_SIZE
               + lax.broadcasted_iota(I32, allk.shape, 2))

        def count(mask):
            return jnp.sum(jnp.sum(mask.astype(I32), axis=0), axis=1, keepdims=True)

        m = jnp.full((t, 1), INT_MIN, I32)
        for bit in range(31, -1, -1):
            cand = (m ^ INT_MIN) if bit == 31 else (m | (1 << bit))
            m = jnp.where(count(allk >= cand[None]) >= k_sel, cand, m)
        need = k_sel - count(allk >= (m + 1)[None])
        tied_idx = jnp.where(allk == m[None], idx, BIG_IDX)
        jcut = jnp.zeros((t, 1), I32)
        for bit in range(idx_bits - 1, -1, -1):
            cand = jcut | (1 << bit)
            below = count(tied_idx < cand[None])
            jcut = jnp.where(below < need, cand, jcut)
        m_ref[0] = jnp.broadcast_to(m, (t, LANES_V7X))
        jc_ref[0] = jnp.broadcast_to(jcut, (t, LANES_V7X))


def _dsa_sample_attend_kernel(pt_ref, kc_ref, vc_ref, skey_ref, m_ref, jc_ref, q_ref, kn_ref, vn_ref,
                              o_ref, qbd_ref, ms_ref, ls_ref, acc_ref, *, n_pages):
    j = pl.program_id(1)
    t = q_ref.shape[1]

    @pl.when(j == 0)
    def _():
        q = q_ref[0] * (HEAD_DIM ** -0.5)
        head = lax.broadcasted_iota(I32, q.shape, 1) // HEAD_DIM
        qbd_ref[...] = jnp.concatenate([jnp.where(head == h, q, 0.0) for h in range(C_HEADS)], axis=0)
        ms_ref[...] = jnp.full(ms_ref.shape, MASKED, F32)
        ls_ref[...] = jnp.zeros(ls_ref.shape, F32)
        acc_ref[...] = jnp.zeros(acc_ref.shape, F32)

    key = skey_ref[0]
    m = m_ref[0]
    jcut = jc_ref[0]
    lane = lax.broadcasted_iota(I32, key.shape, 1)
    row = lax.broadcasted_iota(I32, key.shape, 0)
    kidx = j * PAGE_SIZE + lane
    last_adm = jnp.where(j < n_pages, PAGE_SIZE, row)
    rank = jnp.where(lane <= last_adm, _cut_rank(key, m, kidx), BIG_IDX)
    sel_i = jnp.concatenate([rank] * C_HEADS, axis=0) <= jnp.concatenate([jcut] * C_HEADS, axis=0)

    def step(kp, vp):
        s = lax.dot_general(qbd_ref[...].astype(BF16), kp.astype(BF16), NT_DIMS,
                            preferred_element_type=F32)
        s = jnp.where(sel_i, s, MASKED)
        mprev = ms_ref[:, 0:1]
        mn = jnp.maximum(mprev, jnp.max(s, axis=1, keepdims=True))
        alpha = jnp.exp(mprev - mn)
        p = jnp.exp(s - mn)
        ls_ref[...] = jnp.broadcast_to(alpha * ls_ref[:, 0:1] + jnp.sum(p, axis=1, keepdims=True),
                                       ls_ref.shape)
        acc_ref[...] = alpha * acc_ref[...] + jnp.dot(p.astype(BF16), vp.astype(BF16),
                                                      preferred_element_type=F32)
        ms_ref[...] = jnp.broadcast_to(mn, ms_ref.shape)

    @pl.when(j < n_pages)
    def _():
        step(kc_ref[0, 0], vc_ref[0, 0])

    @pl.when(j == n_pages)
    def _():
        pad = jnp.zeros((PAGE_SIZE - t, C_WIDTH), F32)
        step(jnp.concatenate([kn_ref[0], pad], axis=0), jnp.concatenate([vn_ref[0], pad], axis=0))
        o_all = acc_ref[...] / ls_ref[:, 0:1]
        head = lax.broadcasted_iota(I32, (t, C_WIDTH), 1) // HEAD_DIM
        o = jnp.zeros((t, C_WIDTH), F32)
        for h in range(C_HEADS):
            o = o + jnp.where(head == h, o_all[h * t:(h + 1) * t], 0.0)
        o_ref[0] = o


def _dsa_sample(cache_k, cache_v, cache_kidx, layer, page_table, q, k, v, qi, misc):
    b, t, _ = q.shape
    n_pages = page_table.shape[1]
    past = n_pages * PAGE_SIZE
    k_sel = min(TOPK_MAX, (past + t) // 4)
    idx_bits = max(1, (past + PAGE_SIZE - 1).bit_length())
    width = (n_pages + 1) * PAGE_SIZE
    page = lambda i, j, pt: (layer, pt[i, jnp.minimum(j, n_pages - 1)], 0, 0)
    per_b = lambda wd: pl.BlockSpec((1, t, wd), lambda i, j, pt: (i, 0, 0))
    skey_spec = pl.BlockSpec((1, t, PAGE_SIZE), lambda i, j, pt: (i, 0, j))

    skey, mth, jcut = pl.pallas_call(
        functools.partial(_dsa_sample_index_kernel, k_sel=k_sel, n_pages=n_pages, idx_bits=idx_bits),
        out_shape=(jax.ShapeDtypeStruct((b, t, width), I32),
                   jax.ShapeDtypeStruct((b, t, LANES_V7X), I32),
                   jax.ShapeDtypeStruct((b, t, LANES_V7X), I32)),
        grid_spec=pltpu.PrefetchScalarGridSpec(
            num_scalar_prefetch=1,
            grid=(b, n_pages + 1),
            in_specs=[pl.BlockSpec((1, 1, PAGE_SIZE, IDX_DIM), page), per_b(QI_WIDTH), per_b(MISC_W)],
            out_specs=(skey_spec, per_b(LANES_V7X), per_b(LANES_V7X)),
            scratch_shapes=[pltpu.VMEM((n_pages + 1, t, PAGE_SIZE), I32)]),
        compiler_params=_params("arbitrary", "arbitrary"),
        name="dsa_sample_index",
    )(page_table, cache_kidx, qi, misc)

    rows = C_HEADS * t
    return pl.pallas_call(
        functools.partial(_dsa_sample_attend_kernel, n_pages=n_pages),
        out_shape=jax.ShapeDtypeStruct((b, t, C_WIDTH), F32),
        grid_spec=pltpu.PrefetchScalarGridSpec(
            num_scalar_prefetch=1,
            grid=(b, n_pages + 1),
            in_specs=[pl.BlockSpec((1, 1, PAGE_SIZE, C_WIDTH), page),
                      pl.BlockSpec((1, 1, PAGE_SIZE, C_WIDTH), page),
                      skey_spec, per_b(LANES_V7X), per_b(LANES_V7X),
                      per_b(C_WIDTH), per_b(C_WIDTH), per_b(C_WIDTH)],
            out_specs=per_b(C_WIDTH),
            scratch_shapes=[pltpu.VMEM((rows, C_WIDTH), F32), pltpu.VMEM((rows, LANES_V7X), F32),
                            pltpu.VMEM((rows, LANES_V7X), F32), pltpu.VMEM((rows, C_WIDTH), F32)]),
        compiler_params=_params("arbitrary", "arbitrary"),
        name="dsa_sample_attend",
    )(page_table, cache_k, cache_v, skey, mth, jcut, q, k, v)


def _prep_w_in(w):
    o = np.cumsum([0, 3 * A_WIDTH, A_WIDTH, A_HEADS, A_HEADS, B_WIDTH, C_WIDTH, C_WIDTH, C_WIDTH,
                   QI_WIDTH, IDX_DIM, IDX_HEADS])
    part = lambda n: w[:, o[n]:o[n + 1]]
    qkv, z, bb, aa, u, q, k, v, qi, ki, wi = (part(n) for n in range(11))
    pad = jnp.zeros((w.shape[0], MISC_W - (IDX_DIM + IDX_HEADS + 2 * A_HEADS)), w.dtype)
    return jnp.concatenate([qkv, z, u, q, k, v, qi, ki, wi, bb, aa, pad], axis=1).astype(BF16)


def _rope_tabs(pos):
    inv = ROPE_THETA ** (-jnp.arange(ROT_HALF, dtype=F32) / ROT_HALF)
    ang = pos.astype(F32)[:, None] * inv[None, :]
    cos, sin = jnp.cos(ang), jnp.sin(ang)
    n = pos.shape[0]
    rest = HEAD_DIM - ROT_DIM
    ct = jnp.concatenate([cos, cos, jnp.ones((n, rest), F32)], axis=1)
    sa = jnp.concatenate([-sin, jnp.zeros((n, rest + ROT_HALF), F32)], axis=1)
    sb = jnp.concatenate([jnp.zeros((n, ROT_HALF), F32), sin, jnp.zeros((n, rest), F32)], axis=1)
    rep = LANES_V7X // HEAD_DIM
    return tuple(jnp.tile(a, (1, rep)) for a in (ct, sa, sb))


def _delta_consts():
    lane = np.arange(A_WIDTH)
    seg = (lane[:, None] // HEAD_DIM == lane[None, :] // HEAD_DIM).astype(np.float32)
    src = np.arange(MISC_W)
    selb = (src[:, None] == MISC_B + lane[None, :] // HEAD_DIM).astype(np.float32)
    selg = (src[:, None] == MISC_A + lane[None, :] // HEAD_DIM).astype(np.float32)
    return jnp.asarray(seg), jnp.asarray(selb), jnp.asarray(selg)


def _misc_lanes(vec, off):
    return jnp.zeros((1, MISC_W), F32).at[0, off:off + vec.shape[0]].set(vec.astype(F32))


def _block_diag(pw):
    n = pw.shape[0]
    out = jnp.zeros((n * POOL_GROUP, n * POOL_GROUP), pw.dtype)
    for gi in range(n):
        out = out.at[gi * POOL_GROUP:(gi + 1) * POOL_GROUP, gi * POOL_GROUP:(gi + 1) * POOL_GROUP].set(pw[gi])
    return out


def _layer(x2d, b, t, pos0, tabs, conv_prev, pool_prev, delta_prev, attend, wts, consts, tiles, final):
    (g1, w_in, conv_w, a_log, dt_bias, a_norm, pool_w, pool_scale, w_out, g2, w_gate, w_up, w_down, gf) = wts
    tm, tc, tt = tiles
    qkv, z, u, q, k, v, qi, misc = _in_proj(x2d, g1, w_in, tabs, t, tm)
    r3 = lambda a: a.reshape(b, t, a.shape[-1])
    qkv, z, u, q, k, v, qi, misc = (r3(a) for a in (qkv, z, u, q, k, v, qi, misc))

    cprev8 = jnp.concatenate([jnp.zeros((b, CONV_HALO - (CONV_W - 1), 3 * A_WIDTH), F32), conv_prev], axis=1)
    o_a, delta_new = _delta(qkv, z, misc, cprev8, delta_prev, conv_w, a_log, dt_bias, a_norm, consts,
                            tc, math.gcd(t, DELTA_CHUNK))
    prev16 = jnp.concatenate([jnp.zeros((b, POOL_HALO - POOL_STATE, B_WIDTH), F32), pool_prev], axis=1)
    o_b = _pool(u, prev16, pool_w, pool_scale, pos0, tt)
    o_c = attend(q, k, v, qi, misc)

    flat = lambda a: a.reshape(b * t, a.shape[-1])
    y = _out_ffn(x2d, flat(o_a), flat(o_b), flat(o_c), w_out, g2, w_gate, w_up, w_down, gf, tm, final)
    conv_new = jnp.concatenate([conv_prev, qkv], axis=1)[:, -(CONV_W - 1):]
    pool_new = jnp.concatenate([pool_prev, u], axis=1)[:, -POOL_STATE:]
    state = (k.reshape(b, t, C_HEADS, HEAD_DIM), v.reshape(b, t, C_HEADS, HEAD_DIM),
             misc[:, :, MISC_KI:MISC_KI + IDX_DIM], conv_new, pool_new, delta_new)
    return y, state


def kernel(x_prompt, x_sample, cache_k, cache_v, cache_kidx, state_conv, state_pool, state_delta,
           page_table, norm1, w_in, conv_w, a_log, dt_bias, a_norm, pool_w, pool_scale,
           w_out, norm2, w_gate, w_up, w_down, norm_f):
    depth = w_in.shape[0]
    bp, tp, _ = x_prompt.shape
    bs, ts, _ = x_sample.shape
    n_pages = page_table.shape[1]
    past = n_pages * PAGE_SIZE
    n_pool = cache_k.shape[1]

    tabs_p = _rope_tabs(jnp.arange(tp, dtype=I32))
    tabs_s = _rope_tabs(past + jnp.arange(ts, dtype=I32))
    consts = _delta_consts()
    ck = cache_k.reshape(depth, n_pool, PAGE_SIZE, C_WIDTH)
    cv = cache_v.reshape(depth, n_pool, PAGE_SIZE, C_WIDTH)
    gf = norm_f.reshape(1, D_MODEL)

    tiles_p = (min(512, bp * tp), min(128, tp), min(512, tp))
    tiles_s = (bs * ts, ts, ts)
    conv0 = jnp.zeros((bp, CONV_W - 1, 3 * A_WIDTH), F32)
    pool0 = jnp.zeros((bp, POOL_STATE, B_WIDTH), F32)
    delta0 = jnp.zeros((bp, A_HEADS, HEAD_DIM, HEAD_DIM), F32)

    hp = x_prompt.reshape(bp * tp, D_MODEL)
    hs = x_sample.reshape(bs * ts, D_MODEL)
    p_st, s_st = [], []
    for l in range(depth):
        cw8 = jnp.concatenate([conv_w[l], jnp.zeros((CONV_HALO - CONV_W, 3 * A_WIDTH), F32)], axis=0)
        wts = (norm1[l].reshape(1, D_MODEL), _prep_w_in(w_in[l]), cw8,
               _misc_lanes(a_log[l], MISC_A), _misc_lanes(dt_bias[l], MISC_A),
               a_norm[l].reshape(1, HEAD_DIM), _block_diag(pool_w[l]).astype(BF16),
               pool_scale[l].reshape(1, B_WIDTH), w_out[l].astype(BF16), norm2[l].reshape(1, D_MODEL),
               w_gate[l].astype(BF16), w_up[l].astype(BF16), w_down[l].astype(BF16), gf)
        final = l == depth - 1
        attend_p = functools.partial(_dsa_prompt, tq=LANES_V7X)
        hp, sp = _layer(hp, bp, tp, 0, tabs_p, conv0, pool0, delta0, attend_p, wts, consts, tiles_p, final)
        attend_s = functools.partial(_dsa_sample, ck, cv, cache_kidx, l, page_table)
        hs, ss = _layer(hs, bs, ts, past, tabs_s, state_conv[l], state_pool[l], state_delta[l],
                        attend_s, wts, consts, tiles_s, final)
        p_st.append(sp)
        s_st.append(ss)
    outs = [hp.reshape(bp, tp, D_MODEL), hs.reshape(bs, ts, D_MODEL)]
    for sts in (p_st, s_st):
        for n in range(6):
            outs.append(jnp.stack([s[n] for s in sts]))
    return tuple(outs)
```

```python
import functools
import math

import jax
import jax.numpy as jnp
import numpy as np
from jax import lax
from jax.experimental import pallas as pl
from jax.experimental.pallas import tpu as pltpu

F32, BF16, I32 = jnp.float32, jnp.bfloat16, jnp.int32

D_MODEL = 1024
HEAD_DIM = 64
A_HEADS = 6
A_WIDTH = A_HEADS * HEAD_DIM
CONV_W = 4
DELTA_CHUNK = 64
POOL_WINDOWS = (2, 4, 8, 16)
POOL_GROUP = 64
B_WIDTH = POOL_GROUP * len(POOL_WINDOWS)
POOL_STATE = max(POOL_WINDOWS) - 1
C_HEADS = 6
C_WIDTH = C_HEADS * HEAD_DIM
IDX_HEADS = 4
IDX_DIM = 64
QI_WIDTH = IDX_HEADS * IDX_DIM
TOPK_MAX = 256
ROT_DIM = HEAD_DIM // 4
ROT_HALF = ROT_DIM // 2
ROPE_THETA = 500000.0
D_FF = -(-8 * D_MODEL // (3 * 256)) * 256
EPS = 1e-6
PAGE_SIZE = 128

LANES_V7X = 128
SUBLANES_V7X = 8
VMEM_LIMIT_V7X = 56 * 1024 * 1024

MISC_W = LANES_V7X
MISC_KI, MISC_WI, MISC_B, MISC_A = 0, IDX_DIM, IDX_DIM + IDX_HEADS, IDX_DIM + IDX_HEADS + A_HEADS
OFF_QKV = 0
OFF_Z = OFF_QKV + 3 * A_WIDTH
OFF_U = OFF_Z + A_WIDTH
OFF_Q = OFF_U + B_WIDTH
OFF_K = OFF_Q + C_WIDTH
OFF_V = OFF_K + C_WIDTH
OFF_QI = OFF_V + C_WIDTH
OFF_MISC = OFF_QI + QI_WIDTH
IN_COLS_PAD = OFF_MISC + MISC_W

INT_MIN = -(2 ** 31)
NEG_INF_KEY = -2139095041
MASKED = -1e30
NT_DIMS = (((1,), (1,)), ((), ()))
BIG_IDX = 2 ** 30

_HI = lax.Precision.HIGHEST


def _hdot(a, b):
    return jnp.dot(a, b, precision=_HI, preferred_element_type=F32)


def _sigmoid(x):
    return 1.0 / (1.0 + jnp.exp(-x))


def _softplus(x):
    return jnp.maximum(x, 0.0) + jnp.log1p(jnp.exp(-jnp.abs(x)))


def _score_key(s):
    s = jnp.where(s == 0.0, 0.0, s)
    bits = lax.bitcast_convert_type(s, I32)
    return bits ^ ((bits >> 31) & 0x7FFFFFFF)


def _cut_rank(key, m, kidx):
    return jnp.where(key == m, kidx, jnp.where(key > m, -1, BIG_IDX))


def _params(*sem):
    return pltpu.CompilerParams(dimension_semantics=sem, vmem_limit_bytes=VMEM_LIMIT_V7X)


def _rope(y, c, sa, sb):
    w = y.shape[-1]
    return y * c + pltpu.roll(y, w - ROT_HALF, 1) * sa + pltpu.roll(y, ROT_HALF, 1) * sb


def _in_proj_kernel(x_ref, g_ref, w_ref, ct_ref, sa_ref, sb_ref,
                    qkv_ref, z_ref, u_ref, q_ref, k_ref, v_ref, qi_ref, misc_ref):
    x = x_ref[...]
    ms = jnp.mean(x * x, axis=-1, keepdims=True)
    hn = (x * lax.rsqrt(ms + EPS) * g_ref[...]).astype(BF16)

    def proj(a, b):
        return jnp.dot(hn, w_ref[:, a:b], preferred_element_type=F32)

    qkv_ref[...] = proj(OFF_QKV, OFF_Z)
    z_ref[...] = proj(OFF_Z, OFF_U)
    u_ref[...] = proj(OFF_U, OFF_Q)
    v_ref[...] = proj(OFF_V, OFF_QI)
    ct, sa, sb = ct_ref[...], sa_ref[...], sb_ref[...]

    def tiled(t, n):
        return jnp.concatenate([t] * n, axis=1)

    n3 = C_WIDTH // LANES_V7X
    q_ref[...] = _rope(proj(OFF_Q, OFF_K), tiled(ct, n3), tiled(sa, n3), tiled(sb, n3))
    k_ref[...] = _rope(proj(OFF_K, OFF_V), tiled(ct, n3), tiled(sa, n3), tiled(sb, n3))
    n2 = QI_WIDTH // LANES_V7X
    qi_ref[...] = _rope(proj(OFF_QI, OFF_MISC), tiled(ct, n2), tiled(sa, n2), tiled(sb, n2))
    is_ki = lax.broadcasted_iota(I32, ct.shape, 1) < IDX_DIM
    misc_ref[...] = _rope(proj(OFF_MISC, IN_COLS_PAD), jnp.where(is_ki, ct, 1.0),
                          jnp.where(is_ki, sa, 0.0), jnp.where(is_ki, sb, 0.0))


def _in_proj(x2d, g, w, tabs, seq_len, tm):
    n = x2d.shape[0]
    assert n % tm == 0
    if seq_len % tm == 0:
        per = seq_len // tm
        tab_map = lambda i: (i % per, 0)
    else:
        assert tm % seq_len == 0 and n == tm
        tabs = tuple(jnp.tile(t, (tm // seq_len, 1)) for t in tabs)
        tab_map = lambda i: (0, 0)
    row = lambda width: pl.BlockSpec((tm, width), lambda i: (i, 0))
    const = lambda shape: pl.BlockSpec(shape, lambda i: (0, 0))
    widths = (3 * A_WIDTH, A_WIDTH, B_WIDTH, C_WIDTH, C_WIDTH, C_WIDTH, QI_WIDTH, MISC_W)
    return pl.pallas_call(
        _in_proj_kernel,
        out_shape=tuple(jax.ShapeDtypeStruct((n, wd), F32) for wd in widths),
        grid=(n // tm,),
        in_specs=[row(D_MODEL), const((1, D_MODEL)), const((D_MODEL, IN_COLS_PAD))]
        + [pl.BlockSpec((tm, LANES_V7X), tab_map)] * 3,
        out_specs=tuple(row(wd) for wd in widths),
        compiler_params=_params("arbitrary"),
        name="in_proj",
    )(x2d, g, w, *tabs)


FFN_CHUNK = 256


def _out_ffn_kernel(x_ref, oa_ref, ob_ref, oc_ref, wo_ref, g2_ref, wg_ref, wu_ref, wd_ref, gf_ref,
                    y_ref, *, final):
    def mm(a, b):
        return jnp.dot(a, b, preferred_element_type=F32)

    mix = (mm(oa_ref[...].astype(BF16), wo_ref[0:A_WIDTH, :])
           + mm(ob_ref[...].astype(BF16), wo_ref[A_WIDTH:A_WIDTH + B_WIDTH, :])
           + mm(oc_ref[...].astype(BF16), wo_ref[A_WIDTH + B_WIDTH:, :]))
    x1 = x_ref[...] + mix
    ms = jnp.mean(x1 * x1, axis=-1, keepdims=True)
    hn = (x1 * lax.rsqrt(ms + EPS) * g2_ref[...]).astype(BF16)
    acc = x1
    for c0 in range(0, D_FF, FFN_CHUNK):
        gate = mm(hn, wg_ref[:, c0:c0 + FFN_CHUNK])
        up = mm(hn, wu_ref[:, c0:c0 + FFN_CHUNK])
        hid = (gate * _sigmoid(gate) * up).astype(BF16)
        acc = acc + mm(hid, wd_ref[c0:c0 + FFN_CHUNK, :])
    if final:
        ms2 = jnp.mean(acc * acc, axis=-1, keepdims=True)
        acc = acc * lax.rsqrt(ms2 + EPS) * gf_ref[...]
    y_ref[...] = acc


def _out_ffn(x2d, oa, ob, oc, wo, g2, wg, wu, wd, gf, tm, final):
    n = x2d.shape[0]
    assert n % tm == 0 and D_FF % FFN_CHUNK == 0
    row = lambda width: pl.BlockSpec((tm, width), lambda i: (i, 0))
    const = lambda shape: pl.BlockSpec(shape, lambda i: (0, 0))
    return pl.pallas_call(
        functools.partial(_out_ffn_kernel, final=final),
        out_shape=jax.ShapeDtypeStruct((n, D_MODEL), F32),
        grid=(n // tm,),
        in_specs=[row(D_MODEL), row(A_WIDTH), row(B_WIDTH), row(C_WIDTH),
                  const((D_MODEL, D_MODEL)), const((1, D_MODEL)),
                  const((D_MODEL, D_FF)), const((D_MODEL, D_FF)), const((D_FF, D_MODEL)),
                  const((1, D_MODEL))],
        out_specs=row(D_MODEL),
        compiler_params=_params("arbitrary"),
        name="out_ffn",
    )(x2d, oa, ob, oc, wo, g2, wg, wu, wd, gf)


POOL_HALO = max(POOL_WINDOWS)


def _pool_kernel(u_ref, prev_ref, w_ref, sc_ref, o_ref, xb_ref, *, pos0):
    t = pl.program_id(1)
    tt = u_ref.shape[1]

    @pl.when(t == 0)
    def _():
        xb_ref[0:POOL_HALO, :] = prev_ref[0]

    u = u_ref[0]
    xb_ref[POOL_HALO:POOL_HALO + tt, :] = u
    sums = {}
    s = u
    for j in range(1, POOL_HALO):
        s = s + xb_ref[POOL_HALO - j:POOL_HALO - j + tt, :]
        if j + 1 in POOL_WINDOWS:
            sums[j + 1] = s
    halo = xb_ref[tt:tt + POOL_HALO, :]
    xb_ref[0:POOL_HALO, :] = halo
    pos = pos0 + t * tt + lax.broadcasted_iota(I32, (tt, B_WIDTH), 0)
    group = lax.broadcasted_iota(I32, (tt, B_WIDTH), 1) // POOL_GROUP
    mean = jnp.zeros((tt, B_WIDTH), F32)
    for gi, win in enumerate(POOL_WINDOWS):
        cnt = jnp.minimum(win, pos + 1).astype(F32)
        mean = jnp.where(group == gi, sums[win] / cnt, mean)
    d = (mean - u).astype(BF16)
    o_ref[0] = jnp.dot(d, w_ref[...], preferred_element_type=F32) * sc_ref[...]


def _pool(u, prev16, wbd, scale, pos0, tt):
    b, t, _ = u.shape
    assert t % tt == 0
    return pl.pallas_call(
        functools.partial(_pool_kernel, pos0=pos0),
        out_shape=jax.ShapeDtypeStruct((b, t, B_WIDTH), F32),
        grid=(b, t // tt),
        in_specs=[pl.BlockSpec((1, tt, B_WIDTH), lambda i, j: (i, j, 0)),
                  pl.BlockSpec((1, POOL_HALO, B_WIDTH), lambda i, j: (i, 0, 0)),
                  pl.BlockSpec((B_WIDTH, B_WIDTH), lambda i, j: (0, 0)),
                  pl.BlockSpec((1, B_WIDTH), lambda i, j: (0, 0))],
        out_specs=pl.BlockSpec((1, tt, B_WIDTH), lambda i, j: (i, j, 0)),
        scratch_shapes=[pltpu.VMEM((tt + POOL_HALO, B_WIDTH), F32)],
        compiler_params=_params("arbitrary", "arbitrary"),
        name="pool",
    )(u, prev16, wbd, scale)


CONV_HALO = SUBLANES_V7X


def _delta_kernel(qkv_ref, z_ref, misc_ref, cprev_ref, h0_ref, cw_ref, alog_ref, dtb_ref, an_ref,
                  seg_ref, selb_ref, selg_ref, o_ref, hout_ref, cb_ref, hs_ref, *, chunk):
    t = pl.program_id(1)
    nt = pl.num_programs(1)
    tc = qkv_ref.shape[1]
    c = chunk
    cast = (lambda a: a.astype(BF16)) if c % 16 == 0 else (lambda a: a)

    def mm(a, b):
        return jnp.dot(cast(a), cast(b), preferred_element_type=F32)

    def mm_nt(a, b):
        return lax.dot_general(cast(a), cast(b), NT_DIMS, preferred_element_type=F32)

    @pl.when(t == 0)
    def _():
        cb_ref[0:CONV_HALO, :] = cprev_ref[0]
        hs_ref[...] = h0_ref[0]

    cb_ref[CONV_HALO:CONV_HALO + tc, :] = qkv_ref[0]
    base = CONV_HALO - (CONV_W - 1)
    y = cb_ref[base:base + tc, :] * cw_ref[0:1, :]
    for j in range(1, CONV_W):
        y = y + cb_ref[base + j:base + j + tc, :] * cw_ref[j:j + 1, :]
    tail = cb_ref[tc:tc + CONV_HALO, :]
    cb_ref[0:CONV_HALO, :] = tail
    y = y * _sigmoid(y)
    q, k, v = y[:, 0:A_WIDTH], y[:, A_WIDTH:2 * A_WIDTH], y[:, 2 * A_WIDTH:]
    split = c % 16 == 0

    def pieces(a, n):
        out, r = [], a
        for _ in range(n - 1):
            p = r.astype(BF16)
            out.append(p)
            r = r - p.astype(F32)
        return out + [r.astype(BF16)]

    def bdot(a, b):
        return jnp.dot(a, b, preferred_element_type=F32)

    def dot_sel(a, sel01, n=3):
        if not split:
            return _hdot(a, sel01)
        b = sel01.astype(BF16)
        terms = [bdot(p, b) for p in pieces(a, n)]
        return functools.reduce(lambda x, y: y + x, reversed(terms))

    def sel_dot(sel01, b):
        if not split:
            return _hdot(sel01, b)
        b1, b2, b3 = pieces(b, 3)
        a = sel01.astype(BF16)
        return bdot(a, b1) + (bdot(a, b2) + bdot(a, b3))

    def dot_hl(a, b):
        if not split:
            return _hdot(a, b)
        a1, a2 = pieces(a, 2)
        b1, b2 = pieces(b, 2)
        return bdot(a1, b1) + (bdot(a1, b2) + bdot(a2, b1))

    seg = seg_ref[...]
    qn = q * lax.rsqrt(dot_sel(q * q, seg, 2) + 1e-6) * (HEAD_DIM ** -0.5)
    kn = k * lax.rsqrt(dot_sel(k * k, seg, 2) + 1e-6)
    m = misc_ref[0]
    beta_f = dot_sel(_sigmoid(m), selb_ref[...], 2)
    g_f = dot_sel(-jnp.exp(alog_ref[...]) * _softplus(m + dtb_ref[...]), selg_ref[...])
    z = z_ref[0]

    ri = lax.broadcasted_iota(I32, (c, c), 0)
    ci = lax.broadcasted_iota(I32, (c, c), 1)
    incl = ri >= ci
    strict = ri > ci
    eye = (ri == ci).astype(F32)
    ltri = incl.astype(F32)
    ones_cc = jnp.ones((c, c), F32)
    r64 = lax.broadcasted_iota(I32, (HEAD_DIM, HEAD_DIM), 0)
    c64 = lax.broadcasted_iota(I32, (HEAD_DIM, HEAD_DIM), 1)
    eye64 = (r64 == c64).astype(F32)
    levels = max(int(math.log2(c)) - 1, 0)

    nck = tc // c
    heads = range(A_HEADS)
    rows = [slice(ck * c, (ck + 1) * c) for ck in range(nck)]
    lanes = [slice(h * HEAD_DIM, (h + 1) * HEAD_DIM) for h in heads]
    pairs = [(ck, h) for ck in range(nck) for h in heads]

    gc = [sel_dot(ltri, g_f[r]) for r in rows]
    eg = [jnp.exp(g) for g in gc]
    glast = [g[c - 1:c, :] for g in gc]
    kb = [kn[r] * beta_f[r] for r in rows]
    vb = [v[r] * beta_f[r] for r in rows]
    kbg = [kb[ck] * eg[ck] for ck in range(nck)]
    qg = [qn[r] * eg[ck] for ck, r in enumerate(rows)]
    kdec = [kn[r] * jnp.exp(glast[ck] - gc[ck]) for ck, r in enumerate(rows)]
    hdec = [jnp.exp(g) for g in glast]

    if c == HEAD_DIM:
        rt = lax.broadcasted_iota(I32, (c, A_WIDTH), 0)
        ct = lax.broadcasted_iota(I32, (c, A_WIDTH), 1) % HEAD_DIM
        incl_t = rt >= ct
        eye_t = (rt == ct).astype(F32)
        decay = {}
        for ck in range(nck):
            grow = sel_dot(ones_cc, gc[ck] * eye_t)
            dall = jnp.where(incl_t, jnp.exp(jnp.where(incl_t, gc[ck] - grow, 0.0)), 0.0)
            for h in heads:
                decay[ck, h] = dall[:, lanes[h]]
    else:
        decay = {}
        for ck, h in pairs:
            gcol = gc[ck][:, h * HEAD_DIM:h * HEAD_DIM + c]
            grow = sel_dot(ones_cc, gcol * eye)
            decay[ck, h] = jnp.where(incl, jnp.exp(jnp.where(incl, gcol - grow, 0.0)), 0.0)

    kk = {(ck, h): mm_nt(kb[ck][:, lanes[h]], kn[rows[ck], lanes[h]]) for ck, h in pairs}
    qk = {(ck, h): mm_nt(qn[rows[ck], lanes[h]], kn[rows[ck], lanes[h]]) for ck, h in pairs}
    kdec_t = {(ck, h): mm_nt(eye64, kdec[ck][:, lanes[h]]) for ck, h in pairs}
    lm = {p: jnp.where(strict, kk[p] * decay[p], 0.0) for p in pairs}
    intra = {p: jnp.where(incl, qk[p] * decay[p], 0.0) for p in pairs}
    tinv = {p: eye - lm[p] for p in pairs}
    pw = lm
    for _ in range(levels):
        pw = {p: dot_hl(pw[p], pw[p]) for p in pairs}
        tinv = {p: tinv[p] + dot_hl(tinv[p], pw[p]) for p in pairs}
    u = {(ck, h): dot_hl(tinv[ck, h], vb[ck][:, lanes[h]]) for ck, h in pairs}
    w = {(ck, h): dot_hl(tinv[ck, h], kbg[ck][:, lanes[h]]) for ck, h in pairs}

    for ck in range(nck):
        hst = [hs_ref[h] for h in heads]
        v_new = [u[ck, h] - mm(w[ck, h], hst[h]) for h in heads]
        o_st = [mm(qg[ck][:, lanes[h]], hst[h]) for h in heads]
        o = [o_st[h] + mm(intra[ck, h], v_new[h]) for h in heads]
        for h in heads:
            hs_ref[h] = hst[h] * hdec[ck][:, lanes[h]] + mm(kdec_t[ck, h], v_new[h])
        outs = []
        for h in heads:
            ms = jnp.mean(o[h] * o[h], axis=-1, keepdims=True)
            zh = z[rows[ck], lanes[h]]
            outs.append(o[h] * lax.rsqrt(ms + EPS) * an_ref[...] * (zh * _sigmoid(zh)))
        o_ref[0, rows[ck], :] = jnp.concatenate(outs, axis=1)

    @pl.when(t == nt - 1)
    def _():
        hout_ref[0] = hs_ref[...]


def _delta(qkv, z, misc, cprev8, h0, cw8, alog, dtb, an, consts, tc, chunk):
    b, t, _ = qkv.shape
    assert t % tc == 0 and tc % chunk == 0
    seg, selb, selg = consts
    bt = lambda width: pl.BlockSpec((1, tc, width), lambda i, j: (i, j, 0))
    const = lambda shape: pl.BlockSpec(shape, lambda i, j: tuple(0 for _ in shape))
    hspec = pl.BlockSpec((1, A_HEADS, HEAD_DIM, HEAD_DIM), lambda i, j: (i, 0, 0, 0))
    return pl.pallas_call(
        functools.partial(_delta_kernel, chunk=chunk),
        out_shape=(jax.ShapeDtypeStruct((b, t, A_WIDTH), F32),
                   jax.ShapeDtypeStruct((b, A_HEADS, HEAD_DIM, HEAD_DIM), F32)),
        grid=(b, t // tc),
        in_specs=[bt(3 * A_WIDTH), bt(A_WIDTH), bt(MISC_W),
                  pl.BlockSpec((1, CONV_HALO, 3 * A_WIDTH), lambda i, j: (i, 0, 0)),
                  hspec, const((CONV_HALO, 3 * A_WIDTH)), const((1, MISC_W)), const((1, MISC_W)),
                  const((1, HEAD_DIM)), const((A_WIDTH, A_WIDTH)), const((MISC_W, A_WIDTH)),
                  const((MISC_W, A_WIDTH))],
        out_specs=(bt(A_WIDTH), hspec),
        scratch_shapes=[pltpu.VMEM((tc + CONV_HALO, 3 * A_WIDTH), F32),
                        pltpu.VMEM((A_HEADS, HEAD_DIM, HEAD_DIM), F32)],
        compiler_params=_params("arbitrary", "arbitrary"),
        name="delta",
    )(qkv, z, misc, cprev8, h0, cw8, alog, dtb, an, seg, selb, selg)


KEY_BLOCK = 512
KEY_STEP = 128
COUNT_ROWS = 4 * SUBLANES_V7X


def _dsa_prompt_kernel(q_ref, qi_ref, miscq_ref, ki_ref, kt_ref, vt_ref, o_ref,
                       skey_ref, jc_ref, bias_ref, s_ref, p_ref, m_ref, a_ref, l_ref, acc_ref,
                       *, k_sel, idx_bits, kb):
    i = pl.program_id(1)
    tq = q_ref.shape[1]
    ks = KEY_STEP
    nsub = kb // ks
    nkb = lax.div((i + 1) * tq + (kb - 1), kb)

    wt = miscq_ref[0].T[MISC_WI:MISC_WI + IDX_HEADS, :] * (IDX_DIM ** -0.5 * IDX_HEADS ** -0.5)
    qi = qi_ref[0]
    qs = jnp.concatenate([qi[:, h * IDX_DIM:(h + 1) * IDX_DIM] for h in range(IDX_HEADS)],
                         axis=0).astype(BF16)
    qpos = i * tq + lax.broadcasted_iota(I32, (ks, tq), 1)
    krow = lax.broadcasted_iota(I32, (ks, tq), 0)

    def score_block(jb, carry):
        for u in range(nsub):
            rows = slice(u * ks, (u + 1) * ks)
            d = lax.dot_general(ki_ref[0, jb, rows, :], qs, NT_DIMS, preferred_element_type=F32)
            s = jnp.maximum(d[:, 0:tq], 0.0) * wt[0:1, :]
            for h in range(1, IDX_HEADS):
                s = s + jnp.maximum(d[:, h * tq:(h + 1) * tq], 0.0) * wt[h:h + 1, :]
            skey_ref[jb, rows, :] = _score_key(s)
        return carry

    lax.fori_loop(0, nkb, score_block, 0)
    last = nkb - 1
    for u in range(nsub):
        rows = slice(u * ks, (u + 1) * ks)
        adm = last * kb + u * ks + krow <= qpos
        skey_ref[last, rows, :] = jnp.where(adm, skey_ref[last, rows, :], NEG_INF_KEY)

    def count(pred):
        def body(jb, acc):
            hit = pred(skey_ref[jb], jb).astype(I32).reshape(kb // COUNT_ROWS, COUNT_ROWS, tq)
            return acc + jnp.sum(hit, axis=0)
        acc = lax.fori_loop(0, nkb, body, jnp.zeros((COUNT_ROWS, tq), I32))
        return jnp.sum(acc, axis=0, keepdims=True)

    def count_ge(cand):
        return count(lambda blk, jb: blk >= cand)

    m = jnp.full((1, tq), INT_MIN, I32)
    n_ge = jnp.zeros((1, tq), I32) + nkb * kb
    for bit in range(31, -1, -1):
        cand = (m ^ INT_MIN) if bit == 31 else (m | (1 << bit))
        c = count_ge(cand)
        ok = c >= k_sel
        m = jnp.where(ok, cand, m)
        n_ge = jnp.where(ok, c, n_ge)

    jc_ref[...] = jnp.full(jc_ref.shape, (1 << idx_bits) - 1, I32)

    @pl.when(jnp.max(n_ge) > k_sel)
    def _():
        need = k_sel - count_ge(m + 1)
        brow = lax.broadcasted_iota(I32, (kb, tq), 0)
        jcut = jnp.zeros((1, tq), I32)
        for bit in range(idx_bits - 1, -1, -1):
            cand = jcut | (1 << bit)
            below = count(lambda blk, jb: jnp.where(blk == m, jb * kb + brow, BIG_IDX) < cand)
            jcut = jnp.where(below < need, cand, jcut)
        jc_ref[...] = jnp.broadcast_to(jcut, jc_ref.shape)

    jcut = jc_ref[0:1, :]

    q = q_ref[0] * (HEAD_DIM ** -0.5)
    pair_lane = lax.broadcasted_iota(I32, (tq, LANES_V7X), 1) // HEAD_DIM
    qext = [jnp.where(pair_lane == h % 2, q[:, (h // 2) * LANES_V7X:(h // 2 + 1) * LANES_V7X], 0.0).astype(BF16)
            for h in range(C_HEADS)]
    m_ref[...] = jnp.full(m_ref.shape, MASKED, F32)
    l_ref[...] = jnp.zeros(l_ref.shape, F32)
    acc_ref[...] = jnp.zeros(acc_ref.shape, F32)
    grp = kb // SUBLANES_V7X

    def att_block(jb, carry):
        for u in range(nsub):
            rows = slice(u * ks, (u + 1) * ks)
            kidx = jb * kb + u * ks + krow
            sel = jnp.where(kidx <= qpos, _cut_rank(skey_ref[jb, rows, :], m, kidx), BIG_IDX) <= jcut
            bias_ref[rows, :] = jnp.where(sel, 0.0, MASKED)
        for h in range(C_HEADS):
            g = h // 2
            kt = kt_ref[0, jb, :, g * LANES_V7X:(g + 1) * LANES_V7X]
            s = bias_ref[...] + lax.dot_general(kt, qext[h], NT_DIMS, preferred_element_type=F32)
            s_ref[h] = s
            blk_max = jnp.max(jnp.max(s.reshape(grp, SUBLANES_V7X, tq), axis=0), axis=0, keepdims=True)
            m_old = m_ref[h]
            m_new = jnp.maximum(m_old, blk_max)
            m_ref[h] = m_new
            a_ref[h] = jnp.exp(m_old - m_new)
        for h in range(C_HEADS):
            p = jnp.exp(s_ref[h] - m_ref[h][0:1, :])
            l_ref[h] = a_ref[h] * l_ref[h] + jnp.sum(p.reshape(grp, SUBLANES_V7X, tq), axis=0)
            p_ref[h] = p.astype(BF16)
        for h in range(C_HEADS):
            vt = vt_ref[0, jb, h * HEAD_DIM:(h + 1) * HEAD_DIM, :]
            acc_ref[h] = a_ref[h][0:1, :] * acc_ref[h] + jnp.dot(vt, p_ref[h], preferred_element_type=F32)
        return carry

    lax.fori_loop(0, nkb, att_block, 0)
    o_t = jnp.concatenate([acc_ref[h] / jnp.sum(l_ref[h], axis=0, keepdims=True) for h in range(C_HEADS)],
                          axis=0)
    o_ref[0] = o_t.T


def _dsa_prompt(q, k, v, qi, misc, tq):
    b, t, _ = q.shape
    kb = min(KEY_BLOCK, t)
    assert t % tq == 0 and tq % LANES_V7X == 0 and t % kb == 0 and kb % KEY_STEP == 0
    nb = t // kb
    k_sel = min(TOPK_MAX, t // 4)
    idx_bits = max(1, (t - 1).bit_length())
    ki = misc[:, :, MISC_KI:MISC_KI + IDX_DIM].astype(BF16).reshape(b, nb, kb, IDX_DIM)
    kt = k.reshape(b, nb, kb, C_WIDTH).astype(BF16)
    vt = jnp.swapaxes(v.reshape(b, nb, kb, C_WIDTH), 2, 3).astype(BF16)
    qblk = lambda width: pl.BlockSpec((1, tq, width), lambda i, j: (i, j, 0))
    seq = lambda d2, d3: pl.BlockSpec((1, nb, d2, d3), lambda i, j: (i, 0, 0, 0))
    return pl.pallas_call(
        functools.partial(_dsa_prompt_kernel, k_sel=k_sel, idx_bits=idx_bits, kb=kb),
        out_shape=jax.ShapeDtypeStruct((b, t, C_WIDTH), F32),
        grid=(b, t // tq),
        in_specs=[qblk(C_WIDTH), qblk(QI_WIDTH), qblk(MISC_W),
                  seq(kb, IDX_DIM), seq(kb, C_WIDTH), seq(C_WIDTH, kb)],
        out_specs=qblk(C_WIDTH),
        scratch_shapes=[pltpu.VMEM((nb, kb, tq), I32), pltpu.VMEM((SUBLANES_V7X, tq), I32),
                        pltpu.VMEM((kb, tq), F32),
                        pltpu.VMEM((C_HEADS, kb, tq), F32),
                        pltpu.VMEM((C_HEADS, kb, tq), BF16),
                        pltpu.VMEM((C_HEADS, SUBLANES_V7X, tq), F32),
                        pltpu.VMEM((C_HEADS, SUBLANES_V7X, tq), F32),
                        pltpu.VMEM((C_HEADS, SUBLANES_V7X, tq), F32),
                        pltpu.VMEM((C_HEADS, HEAD_DIM, tq), F32)],
        compiler_params=_params("arbitrary", "arbitrary"),
        name="dsa_prompt",
    )(q, qi, misc, ki, kt, vt)


SAMPLE_PAGES_PER_STEP = 8


def _dsa_sample_index_kernel(pt_ref, *refs, k_sel, n_steps, pps, idx_bits):
    kidx_refs = refs[:pps]
    qi_ref, misc_ref, skp_ref, skn_ref, m_ref, jc_ref, sk_ref = refs[pps:]
    j = pl.program_id(1)
    t = qi_ref.shape[1]
    misc = misc_ref[0]
    qi = qi_ref[0]
    qs = jnp.concatenate([qi[:, h * IDX_DIM:(h + 1) * IDX_DIM] for h in range(IDX_HEADS)],
                         axis=0).astype(BF16)
    ws = jnp.concatenate([misc[:, MISC_WI + h:MISC_WI + h + 1] for h in range(IDX_HEADS)], axis=0)
    ws = ws * (IDX_DIM ** -0.5 * IDX_HEADS ** -0.5)
    lane = lax.broadcasted_iota(I32, (t, PAGE_SIZE), 1)
    row = lax.broadcasted_iota(I32, (t, PAGE_SIZE), 0)

    def keys_of(ki_tiles):
        ds = [lax.dot_general(qs, kt.astype(BF16), NT_DIMS, preferred_element_type=F32) for kt in ki_tiles]
        out = []
        for d in ds:
            sw = jnp.maximum(d, 0.0) * ws
            s = sw[0:t]
            for h in range(1, IDX_HEADS):
                s = s + sw[h * t:(h + 1) * t]
            out.append(_score_key(s))
        return out

    @pl.when(j < n_steps)
    def _():
        for r, key in enumerate(keys_of([kidx_refs[r][0, 0] for r in range(pps)])):
            sk_ref[j * pps + r] = key
            skp_ref[0, :, r * PAGE_SIZE:(r + 1) * PAGE_SIZE] = key

    @pl.when(j == n_steps)
    def _():
        ki_new = misc[:, MISC_KI:MISC_KI + IDX_DIM]
        ki_pad = jnp.concatenate([ki_new, jnp.zeros((PAGE_SIZE - t, IDX_DIM), F32)], axis=0)
        key = jnp.where(lane <= row, keys_of([ki_pad])[0], NEG_INF_KEY)
        sk_ref[n_steps * pps] = key
        skn_ref[0] = key
        allk = sk_ref[...]
        idx = (lax.broadcasted_iota(I32, allk.shape, 0) * PAGE_SIZE
               + lax.broadcasted_iota(I32, allk.shape, 2))

        def count(mask):
            return jnp.sum(jnp.sum(mask.astype(I32), axis=0), axis=1, keepdims=True)

        m = jnp.full((t, 1), INT_MIN, I32)
        for bit in range(31, -1, -1):
            cand = (m ^ INT_MIN) if bit == 31 else (m | (1 << bit))
            m = jnp.where(count(allk >= cand[None]) >= k_sel, cand, m)
        need = k_sel - count(allk >= (m + 1)[None])
        tied_idx = jnp.where(allk == m[None], idx, BIG_IDX)
        jcut = jnp.zeros((t, 1), I32)
        for bit in range(idx_bits - 1, -1, -1):
            cand = jcut | (1 << bit)
            below = count(tied_idx < cand[None])
            jcut = jnp.where(below < need, cand, jcut)
        m_ref[0] = jnp.broadcast_to(m, (t, LANES_V7X))
        jc_ref[0] = jnp.broadcast_to(jcut, (t, LANES_V7X))


def _dsa_sample_attend_kernel(pt_ref, *refs, n_steps, pps):
    kc_refs, vc_refs = refs[:pps], refs[pps:2 * pps]
    (skp_ref, skn_ref, m_ref, jc_ref, q_ref, kn_ref, vn_ref,
     o_ref, qbd_ref, ms_ref, ls_ref, acc_ref) = refs[2 * pps:]
    j = pl.program_id(1)
    t = q_ref.shape[1]

    @pl.when(j == 0)
    def _():
        q = q_ref[0] * (HEAD_DIM ** -0.5)
        head = lax.broadcasted_iota(I32, q.shape, 1) // HEAD_DIM
        qbd_ref[...] = jnp.concatenate([jnp.where(head == h, q, 0.0) for h in range(C_HEADS)], axis=0)
        ms_ref[...] = jnp.full(ms_ref.shape, MASKED, F32)
        ls_ref[...] = jnp.zeros(ls_ref.shape, F32)
        acc_ref[...] = jnp.zeros(acc_ref.shape, F32)

    m = m_ref[0]
    jcut6 = jnp.concatenate([jc_ref[0]] * C_HEADS, axis=0)
    lane = lax.broadcasted_iota(I32, m.shape, 1)
    row = lax.broadcasted_iota(I32, m.shape, 0)

    def step(keys, first_idx, last_adm, kps, vps):
        qbd = qbd_ref[...].astype(BF16)
        ss = [lax.dot_general(qbd, kp.astype(BF16), NT_DIMS, preferred_element_type=F32) for kp in kps]
        masked = []
        for r, s in enumerate(ss):
            rank = jnp.where(lane <= last_adm, _cut_rank(keys[r], m, first_idx + r * PAGE_SIZE + lane), BIG_IDX)
            sel = jnp.concatenate([rank] * C_HEADS, axis=0) <= jcut6
            masked.append(jnp.where(sel, s, MASKED))
        top = functools.reduce(jnp.maximum, masked)
        mprev = ms_ref[:, 0:1]
        mn = jnp.maximum(mprev, jnp.max(top, axis=1, keepdims=True))
        alpha = jnp.exp(mprev - mn)
        ps = [jnp.exp(s - mn) for s in masked]
        psum = functools.reduce(lambda a, b: a + b, ps)
        pv = [jnp.dot(p.astype(BF16), vp.astype(BF16), preferred_element_type=F32) for p, vp in zip(ps, vps)]
        ls_ref[...] = jnp.broadcast_to(alpha * ls_ref[:, 0:1] + jnp.sum(psum, axis=1, keepdims=True),
                                       ls_ref.shape)
        acc_ref[...] = alpha * acc_ref[...] + functools.reduce(lambda a, b: a + b, pv)
        ms_ref[...] = jnp.broadcast_to(mn, ms_ref.shape)

    @pl.when(j < n_steps)
    def _():
        keys = [skp_ref[0, :, r * PAGE_SIZE:(r + 1) * PAGE_SIZE] for r in range(pps)]
        step(keys, j * (pps * PAGE_SIZE), PAGE_SIZE,
             [kc_refs[r][0, 0] for r in range(pps)], [vc_refs[r][0, 0] for r in range(pps)])

    @pl.when(j == n_steps)
    def _():
        pad = jnp.zeros((PAGE_SIZE - t, C_WIDTH), F32)
        step([skn_ref[0]], n_steps * pps * PAGE_SIZE, row,
             [jnp.concatenate([kn_ref[0], pad], axis=0)], [jnp.concatenate([vn_ref[0], pad], axis=0)])
        o_all = acc_ref[...] / ls_ref[:, 0:1]
        head = lax.broadcasted_iota(I32, (t, C_WIDTH), 1) // HEAD_DIM
        o = jnp.zeros((t, C_WIDTH), F32)
        for h in range(C_HEADS):
            o = o + jnp.where(head == h, o_all[h * t:(h + 1) * t], 0.0)
        o_ref[0] = o


def _dsa_sample(cache_k, cache_v, cache_kidx, layer, page_table, q, k, v, qi, misc):
    b, t, _ = q.shape
    n_pages = page_table.shape[1]
    past = n_pages * PAGE_SIZE
    k_sel = min(TOPK_MAX, (past + t) // 4)
    idx_bits = max(1, (past + PAGE_SIZE - 1).bit_length())
    pps = math.gcd(n_pages, SAMPLE_PAGES_PER_STEP)
    n_steps = n_pages // pps

    def page(r):
        return lambda i, j, pt: (layer, pt[i, jnp.minimum(j, n_steps - 1) * pps + r], 0, 0)

    pages = lambda wd: [pl.BlockSpec((1, 1, PAGE_SIZE, wd), page(r)) for r in range(pps)]
    per_b = lambda wd: pl.BlockSpec((1, t, wd), lambda i, j, pt: (i, 0, 0))
    skp_spec = pl.BlockSpec((1, t, pps * PAGE_SIZE), lambda i, j, pt: (i, 0, jnp.minimum(j, n_steps - 1)))

    skp, skn, mth, jcut = pl.pallas_call(
        functools.partial(_dsa_sample_index_kernel, k_sel=k_sel, n_steps=n_steps, pps=pps, idx_bits=idx_bits),
        out_shape=(jax.ShapeDtypeStruct((b, t, past), I32),
                   jax.ShapeDtypeStruct((b, t, PAGE_SIZE), I32),
                   jax.ShapeDtypeStruct((b, t, LANES_V7X), I32),
                   jax.ShapeDtypeStruct((b, t, LANES_V7X), I32)),
        grid_spec=pltpu.PrefetchScalarGridSpec(
            num_scalar_prefetch=1,
            grid=(b, n_steps + 1),
            in_specs=pages(IDX_DIM) + [per_b(QI_WIDTH), per_b(MISC_W)],
            out_specs=(skp_spec, per_b(PAGE_SIZE), per_b(LANES_V7X), per_b(LANES_V7X)),
            scratch_shapes=[pltpu.VMEM((n_pages + 1, t, PAGE_SIZE), I32)]),
        compiler_params=_params("arbitrary", "arbitrary"),
        name="dsa_sample_index",
    )(page_table, *([cache_kidx] * pps), qi, misc)

    rows = C_HEADS * t
    return pl.pallas_call(
        functools.partial(_dsa_sample_attend_kernel, n_steps=n_steps, pps=pps),
        out_shape=jax.ShapeDtypeStruct((b, t, C_WIDTH), F32),
        grid_spec=pltpu.PrefetchScalarGridSpec(
            num_scalar_prefetch=1,
            grid=(b, n_steps + 1),
            in_specs=pages(C_WIDTH) + pages(C_WIDTH)
            + [skp_spec, per_b(PAGE_SIZE), per_b(LANES_V7X), per_b(LANES_V7X),
               per_b(C_WIDTH), per_b(C_WIDTH), per_b(C_WIDTH)],
            out_specs=per_b(C_WIDTH),
            scratch_shapes=[pltpu.VMEM((rows, C_WIDTH), F32), pltpu.VMEM((rows, LANES_V7X), F32),
                            pltpu.VMEM((rows, LANES_V7X), F32), pltpu.VMEM((rows, C_WIDTH), F32)]),
        compiler_params=_params("arbitrary", "arbitrary"),
        name="dsa_sample_attend",
    )(page_table, *([cache_k] * pps), *([cache_v] * pps), skp, skn, mth, jcut, q, k, v)


def _prep_w_in(w):
    o = np.cumsum([0, 3 * A_WIDTH, A_WIDTH, A_HEADS, A_HEADS, B_WIDTH, C_WIDTH, C_WIDTH, C_WIDTH,
                   QI_WIDTH, IDX_DIM, IDX_HEADS])
    part = lambda n: w[:, o[n]:o[n + 1]]
    qkv, z, bb, aa, u, q, k, v, qi, ki, wi = (part(n) for n in range(11))
    pad = jnp.zeros((w.shape[0], MISC_W - (IDX_DIM + IDX_HEADS + 2 * A_HEADS)), w.dtype)
    return jnp.concatenate([qkv, z, u, q, k, v, qi, ki, wi, bb, aa, pad], axis=1).astype(BF16)


def _rope_tabs(pos):
    inv = ROPE_THETA ** (-jnp.arange(ROT_HALF, dtype=F32) / ROT_HALF)
    ang = pos.astype(F32)[:, None] * inv[None, :]
    cos, sin = jnp.cos(ang), jnp.sin(ang)
    n = pos.shape[0]
    rest = HEAD_DIM - ROT_DIM
    ct = jnp.concatenate([cos, cos, jnp.ones((n, rest), F32)], axis=1)
    sa = jnp.concatenate([-sin, jnp.zeros((n, rest + ROT_HALF), F32)], axis=1)
    sb = jnp.concatenate([jnp.zeros((n, ROT_HALF), F32), sin, jnp.zeros((n, rest), F32)], axis=1)
    rep = LANES_V7X // HEAD_DIM
    return tuple(jnp.tile(a, (1, rep)) for a in (ct, sa, sb))


def _delta_consts():
    lane = np.arange(A_WIDTH)
    seg = (lane[:, None] // HEAD_DIM == lane[None, :] // HEAD_DIM).astype(np.float32)
    src = np.arange(MISC_W)
    selb = (src[:, None] == MISC_B + lane[None, :] // HEAD_DIM).astype(np.float32)
    selg = (src[:, None] == MISC_A + lane[None, :] // HEAD_DIM).astype(np.float32)
    return jnp.asarray(seg), jnp.asarray(selb), jnp.asarray(selg)


def _misc_lanes(vec, off):
    return jnp.zeros((1, MISC_W), F32).at[0, off:off + vec.shape[0]].set(vec.astype(F32))


def _block_diag(pw):
    n = pw.shape[0]
    out = jnp.zeros((n * POOL_GROUP, n * POOL_GROUP), pw.dtype)
    for gi in range(n):
        out = out.at[gi * POOL_GROUP:(gi + 1) * POOL_GROUP, gi * POOL_GROUP:(gi + 1) * POOL_GROUP].set(pw[gi])
    return out


def _layer(x2d, b, t, pos0, tabs, conv_prev, pool_prev, delta_prev, attend, wts, consts, tiles, final):
    (g1, w_in, conv_w, a_log, dt_bias, a_norm, pool_w, pool_scale, w_out, g2, w_gate, w_up, w_down, gf) = wts
    tm, tc, tt = tiles
    qkv, z, u, q, k, v, qi, misc = _in_proj(x2d, g1, w_in, tabs, t, tm)
    r3 = lambda a: a.reshape(b, t, a.shape[-1])
    qkv, z, u, q, k, v, qi, misc = (r3(a) for a in (qkv, z, u, q, k, v, qi, misc))

    cprev8 = jnp.concatenate([jnp.zeros((b, CONV_HALO - (CONV_W - 1), 3 * A_WIDTH), F32), conv_prev], axis=1)
    o_a, delta_new = _delta(qkv, z, misc, cprev8, delta_prev, conv_w, a_log, dt_bias, a_norm, consts,
                            tc, math.gcd(t, DELTA_CHUNK))
    prev16 = jnp.concatenate([jnp.zeros((b, POOL_HALO - POOL_STATE, B_WIDTH), F32), pool_prev], axis=1)
    o_b = _pool(u, prev16, pool_w, pool_scale, pos0, tt)
    o_c = attend(q, k, v, qi, misc)

    flat = lambda a: a.reshape(b * t, a.shape[-1])
    y = _out_ffn(x2d, flat(o_a), flat(o_b), flat(o_c), w_out, g2, w_gate, w_up, w_down, gf, tm, final)
    conv_new = jnp.concatenate([conv_prev, qkv], axis=1)[:, -(CONV_W - 1):]
    pool_new = jnp.concatenate([pool_prev, u], axis=1)[:, -POOL_STATE:]
    state = (k.reshape(b, t, C_HEADS, HEAD_DIM), v.reshape(b, t, C_HEADS, HEAD_DIM),
             misc[:, :, MISC_KI:MISC_KI + IDX_DIM], conv_new, pool_new, delta_new)
    return y, state


def kernel(x_prompt, x_sample, cache_k, cache_v, cache_kidx, state_conv, state_pool, state_delta,
           page_table, norm1, w_in, conv_w, a_log, dt_bias, a_norm, pool_w, pool_scale,
           w_out, norm2, w_gate, w_up, w_down, norm_f):
    depth = w_in.shape[0]
    bp, tp, _ = x_prompt.shape
    bs, ts, _ = x_sample.shape
    n_pages = page_table.shape[1]
    past = n_pages * PAGE_SIZE
    n_pool = cache_k.shape[1]

    tabs_p = _rope_tabs(jnp.arange(tp, dtype=I32))
    tabs_s = _rope_tabs(past + jnp.arange(ts, dtype=I32))
    consts = _delta_consts()
    ck = cache_k.reshape(depth, n_pool, PAGE_SIZE, C_WIDTH)
    cv = cache_v.reshape(depth, n_pool, PAGE_SIZE, C_WIDTH)
    gf = norm_f.reshape(1, D_MODEL)

    tiles_p = (min(512, bp * tp), min(128, tp), min(512, tp))
    tiles_s = (bs * ts, ts, ts)
    conv0 = jnp.zeros((bp, CONV_W - 1, 3 * A_WIDTH), F32)
    pool0 = jnp.zeros((bp, POOL_STATE, B_WIDTH), F32)
    delta0 = jnp.zeros((bp, A_HEADS, HEAD_DIM, HEAD_DIM), F32)

    hp = x_prompt.reshape(bp * tp, D_MODEL)
    hs = x_sample.reshape(bs * ts, D_MODEL)
    p_st, s_st = [], []
    for l in range(depth):
        cw8 = jnp.concatenate([conv_w[l], jnp.zeros((CONV_HALO - CONV_W, 3 * A_WIDTH), F32)], axis=0)
        wts = (norm1[l].reshape(1, D_MODEL), _prep_w_in(w_in[l]), cw8,
               _misc_lanes(a_log[l], MISC_A), _misc_lanes(dt_bias[l], MISC_A),
               a_norm[l].reshape(1, HEAD_DIM), _block_diag(pool_w[l]).astype(BF16),
               pool_scale[l].reshape(1, B_WIDTH), w_out[l].astype(BF16), norm2[l].reshape(1, D_MODEL),
               w_gate[l].astype(BF16), w_up[l].astype(BF16), w_down[l].astype(BF16), gf)
        final = l == depth - 1
        attend_p = functools.partial(_dsa_prompt, tq=LANES_V7X)
        hp, sp = _layer(hp, bp, tp, 0, tabs_p, conv0, pool0, delta0, attend_p, wts, consts, tiles_p, final)
        attend_s = functools.partial(_dsa_sample, ck, cv, cache_kidx, l, page_table)
        hs, ss = _layer(hs, bs, ts, past, tabs_s, state_conv[l], state_pool[l], state_delta[l],
                        attend_s, wts, consts, tiles_s, final)
        p_st.append(sp)
        s_st.append(ss)
    outs = [hp.reshape(bp, tp, D_MODEL), hs.reshape(bs, ts, D_MODEL)]
    for sts in (p_st, s_st):
        for n in range(6):
            outs.append(jnp.stack([s[n] for s in sts]))
    return tuple(outs)
```

```python
import functools
import math

import jax
import jax.numpy as jnp
import numpy as np
from jax import lax
from jax.experimental import pallas as pl
from jax.experimental.pallas import tpu as pltpu

F32, BF16, I32 = jnp.float32, jnp.bfloat16, jnp.int32

D_MODEL = 1024
HEAD_DIM = 64
A_HEADS = 6
A_WIDTH = A_HEADS * HEAD_DIM
CONV_W = 4
DELTA_CHUNK = 64
POOL_WINDOWS = (2, 4, 8, 16)
POOL_GROUP = 64
B_WIDTH = POOL_GROUP * len(POOL_WINDOWS)
POOL_STATE = max(POOL_WINDOWS) - 1
C_HEADS = 6
C_WIDTH = C_HEADS * HEAD_DIM
IDX_HEADS = 4
IDX_DIM = 64
QI_WIDTH = IDX_HEADS * IDX_DIM
TOPK_MAX = 256
ROT_DIM = HEAD_DIM // 4
ROT_HALF = ROT_DIM // 2
ROPE_THETA = 500000.0
D_FF = -(-8 * D_MODEL // (3 * 256)) * 256
EPS = 1e-6
PAGE_SIZE = 128

LANES_V7X = 128
SUBLANES_V7X = 8
VMEM_LIMIT_V7X = 56 * 1024 * 1024

MISC_W = LANES_V7X
MISC_KI, MISC_WI, MISC_B, MISC_A = 0, IDX_DIM, IDX_DIM + IDX_HEADS, IDX_DIM + IDX_HEADS + A_HEADS
OFF_QKV = 0
OFF_Z = OFF_QKV + 3 * A_WIDTH
OFF_U = OFF_Z + A_WIDTH
OFF_Q = OFF_U + B_WIDTH
OFF_K = OFF_Q + C_WIDTH
OFF_V = OFF_K + C_WIDTH
OFF_QI = OFF_V + C_WIDTH
OFF_MISC = OFF_QI + QI_WIDTH
IN_COLS_PAD = OFF_MISC + MISC_W

INT_MIN = -(2 ** 31)
NEG_INF_KEY = -2139095041
MASKED = -1e30
NT_DIMS = (((1,), (1,)), ((), ()))
BIG_IDX = 2 ** 30

_HI = lax.Precision.HIGHEST


def _hdot(a, b):
    return jnp.dot(a, b, precision=_HI, preferred_element_type=F32)


def _sigmoid(x):
    return 1.0 / (1.0 + jnp.exp(-x))


def _softplus(x):
    return jnp.maximum(x, 0.0) + jnp.log1p(jnp.exp(-jnp.abs(x)))


def _score_key(s):
    s = jnp.where(s == 0.0, 0.0, s)
    bits = lax.bitcast_convert_type(s, I32)
    return bits ^ ((bits >> 31) & 0x7FFFFFFF)


def _cut_rank(key, m, kidx):
    return jnp.where(key == m, kidx, jnp.where(key > m, -1, BIG_IDX))


def _params(*sem):
    return pltpu.CompilerParams(dimension_semantics=sem, vmem_limit_bytes=VMEM_LIMIT_V7X)


def _rope(y, c, sa, sb):
    w = y.shape[-1]
    return y * c + pltpu.roll(y, w - ROT_HALF, 1) * sa + pltpu.roll(y, ROT_HALF, 1) * sb


def _in_proj_kernel(x_ref, g_ref, w_ref, ct_ref, sa_ref, sb_ref,
                    qkv_ref, z_ref, u_ref, q_ref, k_ref, v_ref, qi_ref, misc_ref):
    x = x_ref[...]
    ms = jnp.mean(x * x, axis=-1, keepdims=True)
    hn = (x * lax.rsqrt(ms + EPS) * g_ref[...]).astype(BF16)

    def proj(a, b):
        return jnp.dot(hn, w_ref[:, a:b], preferred_element_type=F32)

    qkv_ref[...] = proj(OFF_QKV, OFF_Z)
    z_ref[...] = proj(OFF_Z, OFF_U)
    u_ref[...] = proj(OFF_U, OFF_Q)
    v_ref[...] = proj(OFF_V, OFF_QI)
    ct, sa, sb = ct_ref[...], sa_ref[...], sb_ref[...]

    def tiled(t, n):
        return jnp.concatenate([t] * n, axis=1)

    n3 = C_WIDTH // LANES_V7X
    q_ref[...] = _rope(proj(OFF_Q, OFF_K), tiled(ct, n3), tiled(sa, n3), tiled(sb, n3))
    k_ref[...] = _rope(proj(OFF_K, OFF_V), tiled(ct, n3), tiled(sa, n3), tiled(sb, n3))
    n2 = QI_WIDTH // LANES_V7X
    qi_ref[...] = _rope(proj(OFF_QI, OFF_MISC), tiled(ct, n2), tiled(sa, n2), tiled(sb, n2))
    is_ki = lax.broadcasted_iota(I32, ct.shape, 1) < IDX_DIM
    misc_ref[...] = _rope(proj(OFF_MISC, IN_COLS_PAD), jnp.where(is_ki, ct, 1.0),
                          jnp.where(is_ki, sa, 0.0), jnp.where(is_ki, sb, 0.0))


def _in_proj(x2d, g, w, tabs, seq_len, tm):
    n = x2d.shape[0]
    assert n % tm == 0
    if seq_len % tm == 0:
        per = seq_len // tm
        tab_map = lambda i: (i % per, 0)
    else:
        assert tm % seq_len == 0 and n == tm
        tabs = tuple(jnp.tile(t, (tm // seq_len, 1)) for t in tabs)
        tab_map = lambda i: (0, 0)
    row = lambda width: pl.BlockSpec((tm, width), lambda i: (i, 0))
    const = lambda shape: pl.BlockSpec(shape, lambda i: (0, 0))
    widths = (3 * A_WIDTH, A_WIDTH, B_WIDTH, C_WIDTH, C_WIDTH, C_WIDTH, QI_WIDTH, MISC_W)
    return pl.pallas_call(
        _in_proj_kernel,
        out_shape=tuple(jax.ShapeDtypeStruct((n, wd), F32) for wd in widths),
        grid=(n // tm,),
        in_specs=[row(D_MODEL), const((1, D_MODEL)), const((D_MODEL, IN_COLS_PAD))]
        + [pl.BlockSpec((tm, LANES_V7X), tab_map)] * 3,
        out_specs=tuple(row(wd) for wd in widths),
        compiler_params=_params("arbitrary"),
        name="in_proj",
    )(x2d, g, w, *tabs)


FFN_CHUNK = 256


def _out_ffn_kernel(x_ref, oa_ref, ob_ref, oc_ref, wo_ref, g2_ref, wg_ref, wu_ref, wd_ref, gf_ref,
                    y_ref, *, final):
    def mm(a, b):
        return jnp.dot(a, b, preferred_element_type=F32)

    mix = (mm(oa_ref[...].astype(BF16), wo_ref[0:A_WIDTH, :])
           + mm(ob_ref[...].astype(BF16), wo_ref[A_WIDTH:A_WIDTH + B_WIDTH, :])
           + mm(oc_ref[...].astype(BF16), wo_ref[A_WIDTH + B_WIDTH:, :]))
    x1 = x_ref[...] + mix
    ms = jnp.mean(x1 * x1, axis=-1, keepdims=True)
    hn = (x1 * lax.rsqrt(ms + EPS) * g2_ref[...]).astype(BF16)
    acc = x1
    for c0 in range(0, D_FF, FFN_CHUNK):
        gate = mm(hn, wg_ref[:, c0:c0 + FFN_CHUNK])
        up = mm(hn, wu_ref[:, c0:c0 + FFN_CHUNK])
        hid = (gate * _sigmoid(gate) * up).astype(BF16)
        acc = acc + mm(hid, wd_ref[c0:c0 + FFN_CHUNK, :])
    if final:
        ms2 = jnp.mean(acc * acc, axis=-1, keepdims=True)
        acc = acc * lax.rsqrt(ms2 + EPS) * gf_ref[...]
    y_ref[...] = acc


def _out_ffn(x2d, oa, ob, oc, wo, g2, wg, wu, wd, gf, tm, final):
    n = x2d.shape[0]
    assert n % tm == 0 and D_FF % FFN_CHUNK == 0
    row = lambda width: pl.BlockSpec((tm, width), lambda i: (i, 0))
    const = lambda shape: pl.BlockSpec(shape, lambda i: (0, 0))
    return pl.pallas_call(
        functools.partial(_out_ffn_kernel, final=final),
        out_shape=jax.ShapeDtypeStruct((n, D_MODEL), F32),
        grid=(n // tm,),
        in_specs=[row(D_MODEL), row(A_WIDTH), row(B_WIDTH), row(C_WIDTH),
                  const((D_MODEL, D_MODEL)), const((1, D_MODEL)),
                  const((D_MODEL, D_FF)), const((D_MODEL, D_FF)), const((D_FF, D_MODEL)),
                  const((1, D_MODEL))],
        out_specs=row(D_MODEL),
        compiler_params=_params("arbitrary"),
        name="out_ffn",
    )(x2d, oa, ob, oc, wo, g2, wg, wu, wd, gf)


POOL_HALO = max(POOL_WINDOWS)


def _pool_kernel(u_ref, prev_ref, w_ref, sc_ref, o_ref, xb_ref, *, pos0):
    t = pl.program_id(1)
    tt = u_ref.shape[1]

    @pl.when(t == 0)
    def _():
        xb_ref[0:POOL_HALO, :] = prev_ref[0]

    u = u_ref[0]
    xb_ref[POOL_HALO:POOL_HALO + tt, :] = u
    sums = {}
    s = u
    for j in range(1, POOL_HALO):
        s = s + xb_ref[POOL_HALO - j:POOL_HALO - j + tt, :]
        if j + 1 in POOL_WINDOWS:
            sums[j + 1] = s
    halo = xb_ref[tt:tt + POOL_HALO, :]
    xb_ref[0:POOL_HALO, :] = halo
    pos = pos0 + t * tt + lax.broadcasted_iota(I32, (tt, B_WIDTH), 0)
    group = lax.broadcasted_iota(I32, (tt, B_WIDTH), 1) // POOL_GROUP
    mean = jnp.zeros((tt, B_WIDTH), F32)
    for gi, win in enumerate(POOL_WINDOWS):
        cnt = jnp.minimum(win, pos + 1).astype(F32)
        mean = jnp.where(group == gi, sums[win] / cnt, mean)
    d = (mean - u).astype(BF16)
    o_ref[0] = jnp.dot(d, w_ref[...], preferred_element_type=F32) * sc_ref[...]


def _pool(u, prev16, wbd, scale, pos0, tt):
    b, t, _ = u.shape
    assert t % tt == 0
    return pl.pallas_call(
        functools.partial(_pool_kernel, pos0=pos0),
        out_shape=jax.ShapeDtypeStruct((b, t, B_WIDTH), F32),
        grid=(b, t // tt),
        in_specs=[pl.BlockSpec((1, tt, B_WIDTH), lambda i, j: (i, j, 0)),
                  pl.BlockSpec((1, POOL_HALO, B_WIDTH), lambda i, j: (i, 0, 0)),
                  pl.BlockSpec((B_WIDTH, B_WIDTH), lambda i, j: (0, 0)),
                  pl.BlockSpec((1, B_WIDTH), lambda i, j: (0, 0))],
        out_specs=pl.BlockSpec((1, tt, B_WIDTH), lambda i, j: (i, j, 0)),
        scratch_shapes=[pltpu.VMEM((tt + POOL_HALO, B_WIDTH), F32)],
        compiler_params=_params("arbitrary", "arbitrary"),
        name="pool",
    )(u, prev16, wbd, scale)


CONV_HALO = SUBLANES_V7X


def _delta_kernel(qkv_ref, z_ref, misc_ref, cprev_ref, h0_ref, cw_ref, alog_ref, dtb_ref, an_ref,
                  seg_ref, selb_ref, selg_ref, o_ref, hout_ref, cb_ref, hs_ref, *, chunk):
    t = pl.program_id(1)
    nt = pl.num_programs(1)
    tc = qkv_ref.shape[1]
    c = chunk
    cast = (lambda a: a.astype(BF16)) if c % 16 == 0 else (lambda a: a)

    def mm(a, b):
        return jnp.dot(cast(a), cast(b), preferred_element_type=F32)

    def mm_nt(a, b):
        return lax.dot_general(cast(a), cast(b), NT_DIMS, preferred_element_type=F32)

    @pl.when(t == 0)
    def _():
        cb_ref[0:CONV_HALO, :] = cprev_ref[0]
        hs_ref[...] = h0_ref[0]

    cb_ref[CONV_HALO:CONV_HALO + tc, :] = qkv_ref[0]
    base = CONV_HALO - (CONV_W - 1)
    y = cb_ref[base:base + tc, :] * cw_ref[0:1, :]
    for j in range(1, CONV_W):
        y = y + cb_ref[base + j:base + j + tc, :] * cw_ref[j:j + 1, :]
    tail = cb_ref[tc:tc + CONV_HALO, :]
    cb_ref[0:CONV_HALO, :] = tail
    y = y * _sigmoid(y)
    q, k, v = y[:, 0:A_WIDTH], y[:, A_WIDTH:2 * A_WIDTH], y[:, 2 * A_WIDTH:]
    split = c % 16 == 0

    def pieces(a, n):
        out, r = [], a
        for _ in range(n - 1):
            p = r.astype(BF16)
            out.append(p)
            r = r - p.astype(F32)
        return out + [r.astype(BF16)]

    def bdot(a, b):
        return jnp.dot(a, b, preferred_element_type=F32)

    def dot_sel(a, sel01, n=3):
        if not split:
            return _hdot(a, sel01)
        b = sel01.astype(BF16)
        terms = [bdot(p, b) for p in pieces(a, n)]
        return functools.reduce(lambda x, y: y + x, reversed(terms))

    def sel_dot(sel01, b):
        if not split:
            return _hdot(sel01, b)
        b1, b2, b3 = pieces(b, 3)
        a = sel01.astype(BF16)
        return bdot(a, b1) + (bdot(a, b2) + bdot(a, b3))

    def dot_hl(a, b):
        if not split:
            return _hdot(a, b)
        a1, a2 = pieces(a, 2)
        b1, b2 = pieces(b, 2)
        return bdot(a1, b1) + (bdot(a1, b2) + bdot(a2, b1))

    seg = seg_ref[...]
    qn = q * lax.rsqrt(dot_sel(q * q, seg, 2) + 1e-6) * (HEAD_DIM ** -0.5)
    kn = k * lax.rsqrt(dot_sel(k * k, seg, 2) + 1e-6)
    m = misc_ref[0]
    beta_f = dot_sel(_sigmoid(m), selb_ref[...], 2)
    g_f = dot_sel(-jnp.exp(alog_ref[...]) * _softplus(m + dtb_ref[...]), selg_ref[...])
    z = z_ref[0]

    ri = lax.broadcasted_iota(I32, (c, c), 0)
    ci = lax.broadcasted_iota(I32, (c, c), 1)
    incl = ri >= ci
    strict = ri > ci
    eye = (ri == ci).astype(F32)
    ltri = incl.astype(F32)
    ones_cc = jnp.ones((c, c), F32)
    r64 = lax.broadcasted_iota(I32, (HEAD_DIM, HEAD_DIM), 0)
    c64 = lax.broadcasted_iota(I32, (HEAD_DIM, HEAD_DIM), 1)
    eye64 = (r64 == c64).astype(F32)
    levels = max(int(math.log2(c)) - 1, 0)

    nck = tc // c
    heads = range(A_HEADS)
    rows = [slice(ck * c, (ck + 1) * c) for ck in range(nck)]
    lanes = [slice(h * HEAD_DIM, (h + 1) * HEAD_DIM) for h in heads]
    pairs = [(ck, h) for ck in range(nck) for h in heads]

    gc = [sel_dot(ltri, g_f[r]) for r in rows]
    eg = [jnp.exp(g) for g in gc]
    glast = [g[c - 1:c, :] for g in gc]
    kb = [kn[r] * beta_f[r] for r in rows]
    vb = [v[r] * beta_f[r] for r in rows]
    kbg = [kb[ck] * eg[ck] for ck in range(nck)]
    qg = [qn[r] * eg[ck] for ck, r in enumerate(rows)]
    kdec = [kn[r] * jnp.exp(glast[ck] - gc[ck]) for ck, r in enumerate(rows)]
    hdec = [jnp.exp(g) for g in glast]

    if c == HEAD_DIM:
        rt = lax.broadcasted_iota(I32, (c, A_WIDTH), 0)
        ct = lax.broadcasted_iota(I32, (c, A_WIDTH), 1) % HEAD_DIM
        incl_t = rt >= ct
        eye_t = (rt == ct).astype(F32)
        decay = {}
        for ck in range(nck):
            grow = sel_dot(ones_cc, gc[ck] * eye_t)
            dall = jnp.where(incl_t, jnp.exp(jnp.where(incl_t, gc[ck] - grow, 0.0)), 0.0)
            for h in heads:
                decay[ck, h] = dall[:, lanes[h]]
    else:
        decay = {}
        for ck, h in pairs:
            gcol = gc[ck][:, h * HEAD_DIM:h * HEAD_DIM + c]
            grow = sel_dot(ones_cc, gcol * eye)
            decay[ck, h] = jnp.where(incl, jnp.exp(jnp.where(incl, gcol - grow, 0.0)), 0.0)

    kk = {(ck, h): mm_nt(kb[ck][:, lanes[h]], kn[rows[ck], lanes[h]]) for ck, h in pairs}
    qk = {(ck, h): mm_nt(qn[rows[ck], lanes[h]], kn[rows[ck], lanes[h]]) for ck, h in pairs}
    kdec_t = {(ck, h): mm_nt(eye64, kdec[ck][:, lanes[h]]) for ck, h in pairs}
    lm = {p: jnp.where(strict, kk[p] * decay[p], 0.0) for p in pairs}
    intra = {p: jnp.where(incl, qk[p] * decay[p], 0.0) for p in pairs}
    tinv = {p: eye - lm[p] for p in pairs}
    pw = lm
    for _ in range(levels):
        pw = {p: dot_hl(pw[p], pw[p]) for p in pairs}
        tinv = {p: tinv[p] + dot_hl(tinv[p], pw[p]) for p in pairs}
    u = {(ck, h): dot_hl(tinv[ck, h], vb[ck][:, lanes[h]]) for ck, h in pairs}
    w = {(ck, h): dot_hl(tinv[ck, h], kbg[ck][:, lanes[h]]) for ck, h in pairs}

    for ck in range(nck):
        hst = [hs_ref[h] for h in heads]
        v_new = [u[ck, h] - mm(w[ck, h], hst[h]) for h in heads]
        o_st = [mm(qg[ck][:, lanes[h]], hst[h]) for h in heads]
        o = [o_st[h] + mm(intra[ck, h], v_new[h]) for h in heads]
        for h in heads:
            hs_ref[h] = hst[h] * hdec[ck][:, lanes[h]] + mm(kdec_t[ck, h], v_new[h])
        outs = []
        for h in heads:
            ms = jnp.mean(o[h] * o[h], axis=-1, keepdims=True)
            zh = z[rows[ck], lanes[h]]
            outs.append(o[h] * lax.rsqrt(ms + EPS) * an_ref[...] * (zh * _sigmoid(zh)))
        o_ref[0, rows[ck], :] = jnp.concatenate(outs, axis=1)

    @pl.when(t == nt - 1)
    def _():
        hout_ref[0] = hs_ref[...]


def _delta(qkv, z, misc, cprev8, h0, cw8, alog, dtb, an, consts, tc, chunk):
    b, t, _ = qkv.shape
    assert t % tc == 0 and tc % chunk == 0
    seg, selb, selg = consts
    bt = lambda width: pl.BlockSpec((1, tc, width), lambda i, j: (i, j, 0))
    const = lambda shape: pl.BlockSpec(shape, lambda i, j: tuple(0 for _ in shape))
    hspec = pl.BlockSpec((1, A_HEADS, HEAD_DIM, HEAD_DIM), lambda i, j: (i, 0, 0, 0))
    return pl.pallas_call(
        functools.partial(_delta_kernel, chunk=chunk),
        out_shape=(jax.ShapeDtypeStruct((b, t, A_WIDTH), F32),
                   jax.ShapeDtypeStruct((b, A_HEADS, HEAD_DIM, HEAD_DIM), F32)),
        grid=(b, t // tc),
        in_specs=[bt(3 * A_WIDTH), bt(A_WIDTH), bt(MISC_W),
                  pl.BlockSpec((1, CONV_HALO, 3 * A_WIDTH), lambda i, j: (i, 0, 0)),
                  hspec, const((CONV_HALO, 3 * A_WIDTH)), const((1, MISC_W)), const((1, MISC_W)),
                  const((1, HEAD_DIM)), const((A_WIDTH, A_WIDTH)), const((MISC_W, A_WIDTH)),
                  const((MISC_W, A_WIDTH))],
        out_specs=(bt(A_WIDTH), hspec),
        scratch_shapes=[pltpu.VMEM((tc + CONV_HALO, 3 * A_WIDTH), F32),
                        pltpu.VMEM((A_HEADS, HEAD_DIM, HEAD_DIM), F32)],
        compiler_params=_params("arbitrary", "arbitrary"),
        name="delta",
    )(qkv, z, misc, cprev8, h0, cw8, alog, dtb, an, seg, selb, selg)


PROMPT_QUERY_TILE = 2 * LANES_V7X
KEY_BLOCK = 512
KEY_STEP = 128
COUNT_ROWS = 4 * SUBLANES_V7X
BIT_GROUP_ROWS = 32 * SUBLANES_V7X


def _transpose32(words):
    a = list(words)
    j, mask = 16, 0x0000FFFF
    while j:
        k = 0
        while k < 32:
            t = (a[k] ^ (a[k + j] >> j)) & mask
            a[k] = a[k] ^ t
            a[k + j] = a[k + j] ^ (t << j)
            k = (k + j + 1) & ~j
        j >>= 1
        mask ^= (mask << j) & 0xFFFFFFFF
    return a


def _dsa_prompt_kernel(q_ref, qi_ref, miscq_ref, ki_ref, kt_ref, vt_ref, o_ref,
                       skey_ref, jc_ref, planes_ref, active_ref, qs_ref, qext_ref, bias_ref, s_ref, p_ref, m_ref, a_ref, l_ref, acc_ref,
                       *, k_sel, idx_bits, kb):
    i = pl.program_id(1)
    tq = q_ref.shape[1]
    ks = KEY_STEP
    nsub = kb // ks
    nkb = lax.div((i + 1) * tq + (kb - 1), kb)

    wt = miscq_ref[0].T[MISC_WI:MISC_WI + IDX_HEADS, :] * (IDX_DIM ** -0.5 * IDX_HEADS ** -0.5)
    qi = qi_ref[0]
    qs_ref[...] = jnp.concatenate([qi[:, h * IDX_DIM:(h + 1) * IDX_DIM] for h in range(IDX_HEADS)],
                                  axis=0).astype(BF16)
    qpos = i * tq + lax.broadcasted_iota(I32, (ks, tq), 1)
    krow = lax.broadcasted_iota(I32, (ks, tq), 0)

    def score_block(jb, carry):
        for u in range(nsub):
            rows = slice(u * ks, (u + 1) * ks)
            d = lax.dot_general(ki_ref[0, jb, rows, :], qs_ref[...], NT_DIMS, preferred_element_type=F32)
            s = jnp.maximum(d[:, 0:tq], 0.0) * wt[0:1, :]
            for h in range(1, IDX_HEADS):
                s = s + jnp.maximum(d[:, h * tq:(h + 1) * tq], 0.0) * wt[h:h + 1, :]
            skey_ref[jb, rows, :] = _score_key(s)
        return carry

    lax.fori_loop(0, nkb, score_block, 0)
    last = nkb - 1
    for u in range(nsub):
        rows = slice(u * ks, (u + 1) * ks)
        adm = last * kb + u * ks + krow <= qpos
        skey_ref[last, rows, :] = jnp.where(adm, skey_ref[last, rows, :], NEG_INF_KEY)

    def count(pred):
        def body(jb, acc):
            hit = pred(skey_ref[jb], jb).astype(I32).reshape(kb // COUNT_ROWS, COUNT_ROWS, tq)
            return acc + jnp.sum(hit, axis=0)
        acc = lax.fori_loop(0, nkb, body, jnp.zeros((COUNT_ROWS, tq), I32))
        return jnp.sum(acc, axis=0, keepdims=True)

    @pl.when((i == 0) & (pl.program_id(0) == 0))
    def _():
        planes_ref[...] = jnp.zeros(planes_ref.shape, I32)

    gpb = kb // BIT_GROUP_ROWS
    n_groups = planes_ref.shape[1]

    def plane_block(jb, carry):
        for gsub in range(gpb):
            for c0 in range(0, tq, LANES_V7X):
                cols = slice(c0, c0 + LANES_V7X)
                words = [skey_ref[jb, gsub * BIT_GROUP_ROWS + SUBLANES_V7X * k:
                                  gsub * BIT_GROUP_ROWS + SUBLANES_V7X * (k + 1), cols] for k in range(32)]
                for w, word in enumerate(_transpose32(words)):
                    planes_ref[31 - w, jb * gpb + gsub, :, cols] = word
        return carry

    lax.fori_loop(0, nkb, plane_block, 0)

    ng = nkb * gpb
    for g in range(n_groups):
        active_ref[g] = jnp.zeros((SUBLANES_V7X, tq), I32) + jnp.where(g < ng, -1, 0)

    def plane(bit, g):
        return ~planes_ref[bit, g] if bit == 31 else planes_ref[bit, g]

    above = jnp.zeros((1, tq), I32)
    m_bits = jnp.zeros((1, tq), I32)
    keep = None
    for bit in range(31, -2, -1):
        parts = [jnp.zeros((SUBLANES_V7X, tq), I32) for _ in range(4)]
        for g in range(n_groups):
            act = active_ref[g]
            if keep is not None:
                act = act & ~(plane(bit + 1, g) ^ keep)
                active_ref[g] = act
            hit = act & plane(bit, g) if bit >= 0 else act
            parts[g % 4] = parts[g % 4] + lax.population_count(hit)
        cnt = jnp.sum((parts[0] + parts[1]) + (parts[2] + parts[3]), axis=0, keepdims=True)
        if bit < 0:
            n_ge = above + cnt
            break
        take = above + cnt >= k_sel
        above = jnp.where(take, above, above + cnt)
        m_bits = m_bits | jnp.where(take, jnp.int32(INT_MIN if bit == 31 else 1 << bit), 0)
        keep = jnp.broadcast_to(jnp.where(take, -1, 0), (SUBLANES_V7X, tq))
    m = m_bits ^ INT_MIN

    jc_ref[...] = jnp.full(jc_ref.shape, (1 << idx_bits) - 1, I32)

    @pl.when(jnp.max(n_ge) > k_sel)
    def _():
        need = k_sel - above
        brow = lax.broadcasted_iota(I32, (kb, tq), 0)
        jcut = jnp.zeros((1, tq), I32)
        for bit in range(idx_bits - 1, -1, -1):
            cand = jcut | (1 << bit)
            below = count(lambda blk, jb: jnp.where(blk == m, jb * kb + brow, BIG_IDX) < cand)
            jcut = jnp.where(below < need, cand, jcut)
        jc_ref[...] = jnp.broadcast_to(jcut, jc_ref.shape)

    jcut = jc_ref[0:1, :]

    q = q_ref[0] * (HEAD_DIM ** -0.5 * math.log2(math.e))
    pair_lane = lax.broadcasted_iota(I32, (tq, LANES_V7X), 1) // HEAD_DIM
    for h in range(C_HEADS):
        qg = q[:, (h // 2) * LANES_V7X:(h // 2 + 1) * LANES_V7X]
        qext_ref[h] = jnp.where(pair_lane == h % 2, qg, 0.0).astype(BF16)
    m_ref[...] = jnp.full(m_ref.shape, MASKED, F32)
    l_ref[...] = jnp.zeros(l_ref.shape, F32)
    acc_ref[...] = jnp.zeros(acc_ref.shape, F32)
    sgrp = ks // SUBLANES_V7X
    dyn0 = pl.multiple_of(lax.shift_right_logical(i, 24) * ks, ks)

    def att_block(jb, carry):
        for u in range(nsub):
            rows = slice(u * ks, (u + 1) * ks)
            kidx = jb * kb + u * ks + krow
            sel = jnp.where(kidx <= qpos, _cut_rank(skey_ref[jb, rows, :], m, kidx), BIG_IDX) <= jcut
            bias_ref[rows, :] = jnp.where(sel, 0.0, MASKED)
        tops = [None] * C_HEADS
        for u in range(nsub):
            rows = slice(u * ks, (u + 1) * ks)
            for h in range(C_HEADS):
                g = h // 2
                kt = kt_ref[0, jb, rows, g * LANES_V7X:(g + 1) * LANES_V7X]
                s = bias_ref[rows, :] + lax.dot_general(kt, qext_ref[h], NT_DIMS, preferred_element_type=F32)
                s_ref[h, rows, :] = s
                top = jnp.max(s.reshape(sgrp, SUBLANES_V7X, tq), axis=0)
                tops[h] = top if u == 0 else jnp.maximum(tops[h], top)
        for h in range(C_HEADS):
            m_old = m_ref[h]
            m_new = jnp.maximum(m_old, jnp.max(tops[h], axis=0, keepdims=True))
            m_ref[h] = m_new
            a_ref[h] = jnp.exp2(m_old - m_new)
        for h in range(C_HEADS):
            m_new = m_ref[h]
            part = a_ref[h] * l_ref[h]
            for u in range(nsub):
                rows = slice(u * ks, (u + 1) * ks)
                s = s_ref[h, pl.ds(dyn0 + u * ks, ks), :]
                p = jnp.exp2(s.reshape(sgrp, SUBLANES_V7X, tq) - m_new[None])
                part = part + jnp.sum(p, axis=0)
                p_ref[h, rows, :] = p.reshape(ks, tq).astype(BF16)
            l_ref[h] = part
        for h in range(C_HEADS):
            vt = vt_ref[0, jb, h * HEAD_DIM:(h + 1) * HEAD_DIM, :]
            p = p_ref[h, pl.ds(dyn0, kb), :]
            acc_ref[h] = a_ref[h][0:1, :] * acc_ref[h] + jnp.dot(vt, p, preferred_element_type=F32)
        return carry

    lax.fori_loop(0, nkb, att_block, 0)
    o_t = jnp.concatenate([acc_ref[h] / jnp.sum(l_ref[h], axis=0, keepdims=True) for h in range(C_HEADS)],
                          axis=0)
    o_ref[0] = o_t.T


def _dsa_prompt(q, k, v, qi, misc, tq):
    b, t, _ = q.shape
    kb = min(KEY_BLOCK, t)
    assert t % tq == 0 and tq % LANES_V7X == 0 and t % kb == 0 and kb % KEY_STEP == 0
    nb = t // kb
    k_sel = min(TOPK_MAX, t // 4)
    idx_bits = max(1, (t - 1).bit_length())
    ki = misc[:, :, MISC_KI:MISC_KI + IDX_DIM].astype(BF16).reshape(b, nb, kb, IDX_DIM)
    kt = k.reshape(b, nb, kb, C_WIDTH).astype(BF16)
    vt = jnp.swapaxes(v.reshape(b, nb, kb, C_WIDTH), 2, 3).astype(BF16)
    qblk = lambda width: pl.BlockSpec((1, tq, width), lambda i, j: (i, j, 0))
    seq = lambda d2, d3: pl.BlockSpec((1, nb, d2, d3), lambda i, j: (i, 0, 0, 0))
    return pl.pallas_call(
        functools.partial(_dsa_prompt_kernel, k_sel=k_sel, idx_bits=idx_bits, kb=kb),
        out_shape=jax.ShapeDtypeStruct((b, t, C_WIDTH), F32),
        grid=(b, t // tq),
        in_specs=[qblk(C_WIDTH), qblk(QI_WIDTH), qblk(MISC_W),
                  seq(kb, IDX_DIM), seq(kb, C_WIDTH), seq(C_WIDTH, kb)],
        out_specs=qblk(C_WIDTH),
        scratch_shapes=[pltpu.VMEM((nb, kb, tq), I32), pltpu.VMEM((SUBLANES_V7X, tq), I32),
                        pltpu.VMEM((32, t // BIT_GROUP_ROWS, SUBLANES_V7X, tq), I32),
                        pltpu.VMEM((t // BIT_GROUP_ROWS, SUBLANES_V7X, tq), I32),
                        pltpu.VMEM((IDX_HEADS * tq, IDX_DIM), BF16),
                        pltpu.VMEM((C_HEADS, tq, LANES_V7X), BF16),
                        pltpu.VMEM((kb, tq), F32),
                        pltpu.VMEM((C_HEADS, kb, tq), F32),
                        pltpu.VMEM((C_HEADS, kb, tq), BF16),
                        pltpu.VMEM((C_HEADS, SUBLANES_V7X, tq), F32),
                        pltpu.VMEM((C_HEADS, SUBLANES_V7X, tq), F32),
                        pltpu.VMEM((C_HEADS, SUBLANES_V7X, tq), F32),
                        pltpu.VMEM((C_HEADS, HEAD_DIM, tq), F32)],
        compiler_params=_params("arbitrary", "arbitrary"),
        name="dsa_prompt",
    )(q, qi, misc, ki, kt, vt)


SAMPLE_PAGES_PER_STEP = 8


def _dsa_sample_index_kernel(pt_ref, *refs, k_sel, n_steps, pps, idx_bits):
    kidx_refs = refs[:pps]
    qi_ref, misc_ref, skp_ref, skn_ref, m_ref, jc_ref, sk_ref = refs[pps:]
    j = pl.program_id(1)
    t = qi_ref.shape[1]
    misc = misc_ref[0]
    qi = qi_ref[0]
    qs = jnp.concatenate([qi[:, h * IDX_DIM:(h + 1) * IDX_DIM] for h in range(IDX_HEADS)],
                         axis=0).astype(BF16)
    ws = jnp.concatenate([misc[:, MISC_WI + h:MISC_WI + h + 1] for h in range(IDX_HEADS)], axis=0)
    ws = ws * (IDX_DIM ** -0.5 * IDX_HEADS ** -0.5)
    lane = lax.broadcasted_iota(I32, (t, PAGE_SIZE), 1)
    row = lax.broadcasted_iota(I32, (t, PAGE_SIZE), 0)

    def keys_of(ki_tiles):
        ds = [lax.dot_general(qs, kt.astype(BF16), NT_DIMS, preferred_element_type=F32) for kt in ki_tiles]
        out = []
        for d in ds:
            sw = jnp.maximum(d, 0.0) * ws
            s = sw[0:t]
            for h in range(1, IDX_HEADS):
                s = s + sw[h * t:(h + 1) * t]
            out.append(_score_key(s))
        return out

    @pl.when(j < n_steps)
    def _():
        for r, key in enumerate(keys_of([kidx_refs[r][0, 0] for r in range(pps)])):
            sk_ref[j * pps + r] = key
            skp_ref[0, :, r * PAGE_SIZE:(r + 1) * PAGE_SIZE] = key

    @pl.when(j == n_steps)
    def _():
        ki_new = misc[:, MISC_KI:MISC_KI + IDX_DIM]
        ki_pad = jnp.concatenate([ki_new, jnp.zeros((PAGE_SIZE - t, IDX_DIM), F32)], axis=0)
        key = jnp.where(lane <= row, keys_of([ki_pad])[0], NEG_INF_KEY)
        sk_ref[n_steps * pps] = key
        skn_ref[0] = key
        allk = sk_ref[...]
        idx = (lax.broadcasted_iota(I32, allk.shape, 0) * PAGE_SIZE
               + lax.broadcasted_iota(I32, allk.shape, 2))

        def count(mask):
            return jnp.sum(jnp.sum(mask.astype(I32), axis=0), axis=1, keepdims=True)

        m = jnp.full((t, 1), INT_MIN, I32)
        for bit in range(31, -1, -1):
            cand = (m ^ INT_MIN) if bit == 31 else (m | (1 << bit))
            m = jnp.where(count(allk >= cand[None]) >= k_sel, cand, m)
        need = k_sel - count(allk >= (m + 1)[None])
        tied_idx = jnp.where(allk == m[None], idx, BIG_IDX)
        jcut = jnp.zeros((t, 1), I32)
        for bit in range(idx_bits - 1, -1, -1):
            cand = jcut | (1 << bit)
            below = count(tied_idx < cand[None])
            jcut = jnp.where(below < need, cand, jcut)
        m_ref[0] = jnp.broadcast_to(m, (t, LANES_V7X))
        jc_ref[0] = jnp.broadcast_to(jcut, (t, LANES_V7X))


def _dsa_sample_attend_kernel(pt_ref, *refs, n_steps, pps):
    kc_refs, vc_refs = refs[:pps], refs[pps:2 * pps]
    (skp_ref, skn_ref, m_ref, jc_ref, q_ref, kn_ref, vn_ref,
     o_ref, qbd_ref, ms_ref, ls_ref, acc_ref) = refs[2 * pps:]
    j = pl.program_id(1)
    t = q_ref.shape[1]

    @pl.when(j == 0)
    def _():
        q = q_ref[0] * (HEAD_DIM ** -0.5)
        head = lax.broadcasted_iota(I32, q.shape, 1) // HEAD_DIM
        qbd_ref[...] = jnp.concatenate([jnp.where(head == h, q, 0.0) for h in range(C_HEADS)], axis=0)
        ms_ref[...] = jnp.full(ms_ref.shape, MASKED, F32)
        ls_ref[...] = jnp.zeros(ls_ref.shape, F32)
        acc_ref[...] = jnp.zeros(acc_ref.shape, F32)

    m = m_ref[0]
    jcut6 = jnp.concatenate([jc_ref[0]] * C_HEADS, axis=0)
    lane = lax.broadcasted_iota(I32, m.shape, 1)
    row = lax.broadcasted_iota(I32, m.shape, 0)

    def step(keys, first_idx, last_adm, kps, vps):
        qbd = qbd_ref[...].astype(BF16)
        ss = [lax.dot_general(qbd, kp.astype(BF16), NT_DIMS, preferred_element_type=F32) for kp in kps]
        masked = []
        for r, s in enumerate(ss):
            rank = jnp.where(lane <= last_adm, _cut_rank(keys[r], m, first_idx + r * PAGE_SIZE + lane), BIG_IDX)
            sel = jnp.concatenate([rank] * C_HEADS, axis=0) <= jcut6
            masked.append(jnp.where(sel, s, MASKED))
        top = functools.reduce(jnp.maximum, masked)
        mprev = ms_ref[:, 0:1]
        mn = jnp.maximum(mprev, jnp.max(top, axis=1, keepdims=True))
        alpha = jnp.exp(mprev - mn)
        ps = [jnp.exp(s - mn) for s in masked]
        psum = functools.reduce(lambda a, b: a + b, ps)
        pv = [jnp.dot(p.astype(BF16), vp.astype(BF16), preferred_element_type=F32) for p, vp in zip(ps, vps)]
        ls_ref[...] = jnp.broadcast_to(alpha * ls_ref[:, 0:1] + jnp.sum(psum, axis=1, keepdims=True),
                                       ls_ref.shape)
        acc_ref[...] = alpha * acc_ref[...] + functools.reduce(lambda a, b: a + b, pv)
        ms_ref[...] = jnp.broadcast_to(mn, ms_ref.shape)

    @pl.when(j < n_steps)
    def _():
        keys = [skp_ref[0, :, r * PAGE_SIZE:(r + 1) * PAGE_SIZE] for r in range(pps)]
        step(keys, j * (pps * PAGE_SIZE), PAGE_SIZE,
             [kc_refs[r][0, 0] for r in range(pps)], [vc_refs[r][0, 0] for r in range(pps)])

    @pl.when(j == n_steps)
    def _():
        pad = jnp.zeros((PAGE_SIZE - t, C_WIDTH), F32)
        step([skn_ref[0]], n_steps * pps * PAGE_SIZE, row,
             [jnp.concatenate([kn_ref[0], pad], axis=0)], [jnp.concatenate([vn_ref[0], pad], axis=0)])
        o_all = acc_ref[...] / ls_ref[:, 0:1]
        head = lax.broadcasted_iota(I32, (t, C_WIDTH), 1) // HEAD_DIM
        o = jnp.zeros((t, C_WIDTH), F32)
        for h in range(C_HEADS):
            o = o + jnp.where(head == h, o_all[h * t:(h + 1) * t], 0.0)
        o_ref[0] = o


def _dsa_sample(cache_k, cache_v, cache_kidx, layer, page_table, q, k, v, qi, misc):
    b, t, _ = q.shape
    n_pages = page_table.shape[1]
    past = n_pages * PAGE_SIZE
    k_sel = min(TOPK_MAX, (past + t) // 4)
    idx_bits = max(1, (past + PAGE_SIZE - 1).bit_length())
    pps = math.gcd(n_pages, SAMPLE_PAGES_PER_STEP)
    n_steps = n_pages // pps

    def page(r):
        return lambda i, j, pt: (layer, pt[i, jnp.minimum(j, n_steps - 1) * pps + r], 0, 0)

    pages = lambda wd: [pl.BlockSpec((1, 1, PAGE_SIZE, wd), page(r)) for r in range(pps)]
    per_b = lambda wd: pl.BlockSpec((1, t, wd), lambda i, j, pt: (i, 0, 0))
    skp_spec = pl.BlockSpec((1, t, pps * PAGE_SIZE), lambda i, j, pt: (i, 0, jnp.minimum(j, n_steps - 1)))

    skp, skn, mth, jcut = pl.pallas_call(
        functools.partial(_dsa_sample_index_kernel, k_sel=k_sel, n_steps=n_steps, pps=pps, idx_bits=idx_bits),
        out_shape=(jax.ShapeDtypeStruct((b, t, past), I32),
                   jax.ShapeDtypeStruct((b, t, PAGE_SIZE), I32),
                   jax.ShapeDtypeStruct((b, t, LANES_V7X), I32),
                   jax.ShapeDtypeStruct((b, t, LANES_V7X), I32)),
        grid_spec=pltpu.PrefetchScalarGridSpec(
            num_scalar_prefetch=1,
            grid=(b, n_steps + 1),
            in_specs=pages(IDX_DIM) + [per_b(QI_WIDTH), per_b(MISC_W)],
            out_specs=(skp_spec, per_b(PAGE_SIZE), per_b(LANES_V7X), per_b(LANES_V7X)),
            scratch_shapes=[pltpu.VMEM((n_pages + 1, t, PAGE_SIZE), I32)]),
        compiler_params=_params("arbitrary", "arbitrary"),
        name="dsa_sample_index",
    )(page_table, *([cache_kidx] * pps), qi, misc)

    rows = C_HEADS * t
    return pl.pallas_call(
        functools.partial(_dsa_sample_attend_kernel, n_steps=n_steps, pps=pps),
        out_shape=jax.ShapeDtypeStruct((b, t, C_WIDTH), F32),
        grid_spec=pltpu.PrefetchScalarGridSpec(
            num_scalar_prefetch=1,
            grid=(b, n_steps + 1),
            in_specs=pages(C_WIDTH) + pages(C_WIDTH)
            + [skp_spec, per_b(PAGE_SIZE), per_b(LANES_V7X), per_b(LANES_V7X),
               per_b(C_WIDTH), per_b(C_WIDTH), per_b(C_WIDTH)],
            out_specs=per_b(C_WIDTH),
            scratch_shapes=[pltpu.VMEM((rows, C_WIDTH), F32), pltpu.VMEM((rows, LANES_V7X), F32),
                            pltpu.VMEM((rows, LANES_V7X), F32), pltpu.VMEM((rows, C_WIDTH), F32)]),
        compiler_params=_params("arbitrary", "arbitrary"),
        name="dsa_sample_attend",
    )(page_table, *([cache_k] * pps), *([cache_v] * pps), skp, skn, mth, jcut, q, k, v)


def _prep_w_in(w):
    o = np.cumsum([0, 3 * A_WIDTH, A_WIDTH, A_HEADS, A_HEADS, B_WIDTH, C_WIDTH, C_WIDTH, C_WIDTH,
                   QI_WIDTH, IDX_DIM, IDX_HEADS])
    part = lambda n: w[:, o[n]:o[n + 1]]
    qkv, z, bb, aa, u, q, k, v, qi, ki, wi = (part(n) for n in range(11))
    pad = jnp.zeros((w.shape[0], MISC_W - (IDX_DIM + IDX_HEADS + 2 * A_HEADS)), w.dtype)
    return jnp.concatenate([qkv, z, u, q, k, v, qi, ki, wi, bb, aa, pad], axis=1).astype(BF16)


def _rope_tabs(pos):
    inv = ROPE_THETA ** (-jnp.arange(ROT_HALF, dtype=F32) / ROT_HALF)
    ang = pos.astype(F32)[:, None] * inv[None, :]
    cos, sin = jnp.cos(ang), jnp.sin(ang)
    n = pos.shape[0]
    rest = HEAD_DIM - ROT_DIM
    ct = jnp.concatenate([cos, cos, jnp.ones((n, rest), F32)], axis=1)
    sa = jnp.concatenate([-sin, jnp.zeros((n, rest + ROT_HALF), F32)], axis=1)
    sb = jnp.concatenate([jnp.zeros((n, ROT_HALF), F32), sin, jnp.zeros((n, rest), F32)], axis=1)
    rep = LANES_V7X // HEAD_DIM
    return tuple(jnp.tile(a, (1, rep)) for a in (ct, sa, sb))


def _delta_consts():
    lane = np.arange(A_WIDTH)
    seg = (lane[:, None] // HEAD_DIM == lane[None, :] // HEAD_DIM).astype(np.float32)
    src = np.arange(MISC_W)
    selb = (src[:, None] == MISC_B + lane[None, :] // HEAD_DIM).astype(np.float32)
    selg = (src[:, None] == MISC_A + lane[None, :] // HEAD_DIM).astype(np.float32)
    return jnp.asarray(seg), jnp.asarray(selb), jnp.asarray(selg)


def _misc_lanes(vec, off):
    return jnp.zeros((1, MISC_W), F32).at[0, off:off + vec.shape[0]].set(vec.astype(F32))


def _block_diag(pw):
    n = pw.shape[0]
    out = jnp.zeros((n * POOL_GROUP, n * POOL_GROUP), pw.dtype)
    for gi in range(n):
        out = out.at[gi * POOL_GROUP:(gi + 1) * POOL_GROUP, gi * POOL_GROUP:(gi + 1) * POOL_GROUP].set(pw[gi])
    return out


def _layer(x2d, b, t, pos0, tabs, conv_prev, pool_prev, delta_prev, attend, wts, consts, tiles, final):
    (g1, w_in, conv_w, a_log, dt_bias, a_norm, pool_w, pool_scale, w_out, g2, w_gate, w_up, w_down, gf) = wts
    tm, tc, tt = tiles
    qkv, z, u, q, k, v, qi, misc = _in_proj(x2d, g1, w_in, tabs, t, tm)
    r3 = lambda a: a.reshape(b, t, a.shape[-1])
    qkv, z, u, q, k, v, qi, misc = (r3(a) for a in (qkv, z, u, q, k, v, qi, misc))

    cprev8 = jnp.concatenate([jnp.zeros((b, CONV_HALO - (CONV_W - 1), 3 * A_WIDTH), F32), conv_prev], axis=1)
    o_a, delta_new = _delta(qkv, z, misc, cprev8, delta_prev, conv_w, a_log, dt_bias, a_norm, consts,
                            tc, math.gcd(t, DELTA_CHUNK))
    prev16 = jnp.concatenate([jnp.zeros((b, POOL_HALO - POOL_STATE, B_WIDTH), F32), pool_prev], axis=1)
    o_b = _pool(u, prev16, pool_w, pool_scale, pos0, tt)
    o_c = attend(q, k, v, qi, misc)

    flat = lambda a: a.reshape(b * t, a.shape[-1])
    y = _out_ffn(x2d, flat(o_a), flat(o_b), flat(o_c), w_out, g2, w_gate, w_up, w_down, gf, tm, final)
    conv_new = jnp.concatenate([conv_prev, qkv], axis=1)[:, -(CONV_W - 1):]
    pool_new = jnp.concatenate([pool_prev, u], axis=1)[:, -POOL_STATE:]
    state = (k.reshape(b, t, C_HEADS, HEAD_DIM), v.reshape(b, t, C_HEADS, HEAD_DIM),
             misc[:, :, MISC_KI:MISC_KI + IDX_DIM], conv_new, pool_new, delta_new)
    return y, state


def kernel(x_prompt, x_sample, cache_k, cache_v, cache_kidx, state_conv, state_pool, state_delta,
           page_table, norm1, w_in, conv_w, a_log, dt_bias, a_norm, pool_w, pool_scale,
           w_out, norm2, w_gate, w_up, w_down, norm_f):
    depth = w_in.shape[0]
    bp, tp, _ = x_prompt.shape
    bs, ts, _ = x_sample.shape
    n_pages = page_table.shape[1]
    past = n_pages * PAGE_SIZE
    n_pool = cache_k.shape[1]

    tabs_p = _rope_tabs(jnp.arange(tp, dtype=I32))
    tabs_s = _rope_tabs(past + jnp.arange(ts, dtype=I32))
    consts = _delta_consts()
    ck = cache_k.reshape(depth, n_pool, PAGE_SIZE, C_WIDTH)
    cv = cache_v.reshape(depth, n_pool, PAGE_SIZE, C_WIDTH)
    gf = norm_f.reshape(1, D_MODEL)

    tiles_p = (min(512, bp * tp), min(128, tp), min(512, tp))
    tiles_s = (bs * ts, ts, ts)
    conv0 = jnp.zeros((bp, CONV_W - 1, 3 * A_WIDTH), F32)
    pool0 = jnp.zeros((bp, POOL_STATE, B_WIDTH), F32)
    delta0 = jnp.zeros((bp, A_HEADS, HEAD_DIM, HEAD_DIM), F32)

    hp = x_prompt.reshape(bp * tp, D_MODEL)
    hs = x_sample.reshape(bs * ts, D_MODEL)
    p_st, s_st = [], []
    for l in range(depth):
        cw8 = jnp.concatenate([conv_w[l], jnp.zeros((CONV_HALO - CONV_W, 3 * A_WIDTH), F32)], axis=0)
        wts = (norm1[l].reshape(1, D_MODEL), _prep_w_in(w_in[l]), cw8,
               _misc_lanes(a_log[l], MISC_A), _misc_lanes(dt_bias[l], MISC_A),
               a_norm[l].reshape(1, HEAD_DIM), _block_diag(pool_w[l]).astype(BF16),
               pool_scale[l].reshape(1, B_WIDTH), w_out[l].astype(BF16), norm2[l].reshape(1, D_MODEL),
               w_gate[l].astype(BF16), w_up[l].astype(BF16), w_down[l].astype(BF16), gf)
        final = l == depth - 1
        attend_p = functools.partial(_dsa_prompt, tq=math.gcd(tp, PROMPT_QUERY_TILE))
        hp, sp = _layer(hp, bp, tp, 0, tabs_p, conv0, pool0, delta0, attend_p, wts, consts, tiles_p, final)
        attend_s = functools.partial(_dsa_sample, ck, cv, cache_kidx, l, page_table)
        hs, ss = _layer(hs, bs, ts, past, tabs_s, state_conv[l], state_pool[l], state_delta[l],
                        attend_s, wts, consts, tiles_s, final)
        p_st.append(sp)
        s_st.append(ss)
    outs = [hp.reshape(bp, tp, D_MODEL), hs.reshape(bs, ts, D_MODEL)]
    for sts in (p_st, s_st):
        for n in range(6):
            outs.append(jnp.stack([s[n] for s in sts]))
    return tuple(outs)
```

```python
import functools
import math

import jax
import jax.numpy as jnp
import numpy as np
from jax import lax
from jax.experimental import pallas as pl
from jax.experimental.pallas import tpu as pltpu

F32, BF16, I32 = jnp.float32, jnp.bfloat16, jnp.int32

D_MODEL = 1024
HEAD_DIM = 64
A_HEADS = 6
A_WIDTH = A_HEADS * HEAD_DIM
CONV_W = 4
DELTA_CHUNK = 64
POOL_WINDOWS = (2, 4, 8, 16)
POOL_GROUP = 64
B_WIDTH = POOL_GROUP * len(POOL_WINDOWS)
POOL_STATE = max(POOL_WINDOWS) - 1
C_HEADS = 6
C_WIDTH = C_HEADS * HEAD_DIM
IDX_HEADS = 4
IDX_DIM = 64
QI_WIDTH = IDX_HEADS * IDX_DIM
TOPK_MAX = 256
ROT_DIM = HEAD_DIM // 4
ROT_HALF = ROT_DIM // 2
ROPE_THETA = 500000.0
D_FF = -(-8 * D_MODEL // (3 * 256)) * 256
EPS = 1e-6
PAGE_SIZE = 128

LANES_V7X = 128
SUBLANES_V7X = 8
VMEM_LIMIT_V7X = 56 * 1024 * 1024

MISC_W = LANES_V7X
MISC_KI, MISC_WI, MISC_B, MISC_A = 0, IDX_DIM, IDX_DIM + IDX_HEADS, IDX_DIM + IDX_HEADS + A_HEADS
OFF_QKV = 0
OFF_Z = OFF_QKV + 3 * A_WIDTH
OFF_U = OFF_Z + A_WIDTH
OFF_Q = OFF_U + B_WIDTH
OFF_K = OFF_Q + C_WIDTH
OFF_V = OFF_K + C_WIDTH
OFF_QI = OFF_V + C_WIDTH
OFF_MISC = OFF_QI + QI_WIDTH
IN_COLS_PAD = OFF_MISC + MISC_W

INT_MIN = -(2 ** 31)
NEG_INF_KEY = -2139095041
MASKED = -1e30
NT_DIMS = (((1,), (1,)), ((), ()))
BIG_IDX = 2 ** 30

_HI = lax.Precision.HIGHEST


def _hdot(a, b):
    return jnp.dot(a, b, precision=_HI, preferred_element_type=F32)


def _sigmoid(x):
    return 1.0 / (1.0 + jnp.exp(-x))


def _softplus(x):
    return jnp.maximum(x, 0.0) + jnp.log1p(jnp.exp(-jnp.abs(x)))


def _score_key(s):
    s = jnp.where(s == 0.0, 0.0, s)
    bits = lax.bitcast_convert_type(s, I32)
    return bits ^ ((bits >> 31) & 0x7FFFFFFF)


def _cut_rank(key, m, kidx):
    return jnp.where(key == m, kidx, jnp.where(key > m, -1, BIG_IDX))


def _params(*sem):
    return pltpu.CompilerParams(dimension_semantics=sem, vmem_limit_bytes=VMEM_LIMIT_V7X)


def _rope(y, c, sa, sb):
    w = y.shape[-1]
    return y * c + pltpu.roll(y, w - ROT_HALF, 1) * sa + pltpu.roll(y, ROT_HALF, 1) * sb


def _in_proj_kernel(x_ref, g_ref, w_ref, ct_ref, sa_ref, sb_ref,
                    qkv_ref, z_ref, u_ref, q_ref, k_ref, v_ref, qi_ref, misc_ref):
    x = x_ref[...]
    ms = jnp.mean(x * x, axis=-1, keepdims=True)
    hn = (x * lax.rsqrt(ms + EPS) * g_ref[...]).astype(BF16)

    def proj(a, b):
        return jnp.dot(hn, w_ref[:, a:b], preferred_element_type=F32)

    qkv_ref[...] = proj(OFF_QKV, OFF_Z)
    z_ref[...] = proj(OFF_Z, OFF_U)
    u_ref[...] = proj(OFF_U, OFF_Q)
    v_ref[...] = proj(OFF_V, OFF_QI)
    ct, sa, sb = ct_ref[...], sa_ref[...], sb_ref[...]

    def tiled(t, n):
        return jnp.concatenate([t] * n, axis=1)

    n3 = C_WIDTH // LANES_V7X
    q_ref[...] = _rope(proj(OFF_Q, OFF_K), tiled(ct, n3), tiled(sa, n3), tiled(sb, n3))
    k_ref[...] = _rope(proj(OFF_K, OFF_V), tiled(ct, n3), tiled(sa, n3), tiled(sb, n3))
    n2 = QI_WIDTH // LANES_V7X
    qi_ref[...] = _rope(proj(OFF_QI, OFF_MISC), tiled(ct, n2), tiled(sa, n2), tiled(sb, n2))
    is_ki = lax.broadcasted_iota(I32, ct.shape, 1) < IDX_DIM
    misc_ref[...] = _rope(proj(OFF_MISC, IN_COLS_PAD), jnp.where(is_ki, ct, 1.0),
                          jnp.where(is_ki, sa, 0.0), jnp.where(is_ki, sb, 0.0))


def _in_proj(x2d, g, w, tabs, seq_len, tm):
    n = x2d.shape[0]
    assert n % tm == 0
    if seq_len % tm == 0:
        per = seq_len // tm
        tab_map = lambda i: (i % per, 0)
    else:
        assert tm % seq_len == 0 and n == tm
        tabs = tuple(jnp.tile(t, (tm // seq_len, 1)) for t in tabs)
        tab_map = lambda i: (0, 0)
    row = lambda width: pl.BlockSpec((tm, width), lambda i: (i, 0))
    const = lambda shape: pl.BlockSpec(shape, lambda i: (0, 0))
    widths = (3 * A_WIDTH, A_WIDTH, B_WIDTH, C_WIDTH, C_WIDTH, C_WIDTH, QI_WIDTH, MISC_W)
    return pl.pallas_call(
        _in_proj_kernel,
        out_shape=tuple(jax.ShapeDtypeStruct((n, wd), F32) for wd in widths),
        grid=(n // tm,),
        in_specs=[row(D_MODEL), const((1, D_MODEL)), const((D_MODEL, IN_COLS_PAD))]
        + [pl.BlockSpec((tm, LANES_V7X), tab_map)] * 3,
        out_specs=tuple(row(wd) for wd in widths),
        compiler_params=_params("arbitrary"),
        name="in_proj",
    )(x2d, g, w, *tabs)


FFN_CHUNK = 256


def _out_ffn_kernel(x_ref, oa_ref, ob_ref, oc_ref, wo_ref, g2_ref, wg_ref, wu_ref, wd_ref, gf_ref,
                    y_ref, *, final):
    def mm(a, b):
        return jnp.dot(a, b, preferred_element_type=F32)

    mix = (mm(oa_ref[...].astype(BF16), wo_ref[0:A_WIDTH, :])
           + mm(ob_ref[...].astype(BF16), wo_ref[A_WIDTH:A_WIDTH + B_WIDTH, :])
           + mm(oc_ref[...].astype(BF16), wo_ref[A_WIDTH + B_WIDTH:, :]))
    x1 = x_ref[...] + mix
    ms = jnp.mean(x1 * x1, axis=-1, keepdims=True)
    hn = (x1 * lax.rsqrt(ms + EPS) * g2_ref[...]).astype(BF16)
    acc = x1
    for c0 in range(0, D_FF, FFN_CHUNK):
        gate = mm(hn, wg_ref[:, c0:c0 + FFN_CHUNK])
        up = mm(hn, wu_ref[:, c0:c0 + FFN_CHUNK])
        hid = (gate * _sigmoid(gate) * up).astype(BF16)
        acc = acc + mm(hid, wd_ref[c0:c0 + FFN_CHUNK, :])
    if final:
        ms2 = jnp.mean(acc * acc, axis=-1, keepdims=True)
        acc = acc * lax.rsqrt(ms2 + EPS) * gf_ref[...]
    y_ref[...] = acc


def _out_ffn(x2d, oa, ob, oc, wo, g2, wg, wu, wd, gf, tm, final):
    n = x2d.shape[0]
    assert n % tm == 0 and D_FF % FFN_CHUNK == 0
    row = lambda width: pl.BlockSpec((tm, width), lambda i: (i, 0))
    const = lambda shape: pl.BlockSpec(shape, lambda i: (0, 0))
    return pl.pallas_call(
        functools.partial(_out_ffn_kernel, final=final),
        out_shape=jax.ShapeDtypeStruct((n, D_MODEL), F32),
        grid=(n // tm,),
        in_specs=[row(D_MODEL), row(A_WIDTH), row(B_WIDTH), row(C_WIDTH),
                  const((D_MODEL, D_MODEL)), const((1, D_MODEL)),
                  const((D_MODEL, D_FF)), const((D_MODEL, D_FF)), const((D_FF, D_MODEL)),
                  const((1, D_MODEL))],
        out_specs=row(D_MODEL),
        compiler_params=_params("arbitrary"),
        name="out_ffn",
    )(x2d, oa, ob, oc, wo, g2, wg, wu, wd, gf)


POOL_HALO = max(POOL_WINDOWS)


def _pool_kernel(u_ref, prev_ref, w_ref, sc_ref, o_ref, xb_ref, *, pos0):
    t = pl.program_id(1)
    tt = u_ref.shape[1]

    @pl.when(t == 0)
    def _():
        xb_ref[0:POOL_HALO, :] = prev_ref[0]

    u = u_ref[0]
    xb_ref[POOL_HALO:POOL_HALO + tt, :] = u
    sums = {}
    s = u
    for j in range(1, POOL_HALO):
        s = s + xb_ref[POOL_HALO - j:POOL_HALO - j + tt, :]
        if j + 1 in POOL_WINDOWS:
            sums[j + 1] = s
    halo = xb_ref[tt:tt + POOL_HALO, :]
    xb_ref[0:POOL_HALO, :] = halo
    pos = pos0 + t * tt + lax.broadcasted_iota(I32, (tt, B_WIDTH), 0)
    group = lax.broadcasted_iota(I32, (tt, B_WIDTH), 1) // POOL_GROUP
    mean = jnp.zeros((tt, B_WIDTH), F32)
    for gi, win in enumerate(POOL_WINDOWS):
        cnt = jnp.minimum(win, pos + 1).astype(F32)
        mean = jnp.where(group == gi, sums[win] / cnt, mean)
    d = (mean - u).astype(BF16)
    o_ref[0] = jnp.dot(d, w_ref[...], preferred_element_type=F32) * sc_ref[...]


def _pool(u, prev16, wbd, scale, pos0, tt):
    b, t, _ = u.shape
    assert t % tt == 0
    return pl.pallas_call(
        functools.partial(_pool_kernel, pos0=pos0),
        out_shape=jax.ShapeDtypeStruct((b, t, B_WIDTH), F32),
        grid=(b, t // tt),
        in_specs=[pl.BlockSpec((1, tt, B_WIDTH), lambda i, j: (i, j, 0)),
                  pl.BlockSpec((1, POOL_HALO, B_WIDTH), lambda i, j: (i, 0, 0)),
                  pl.BlockSpec((B_WIDTH, B_WIDTH), lambda i, j: (0, 0)),
                  pl.BlockSpec((1, B_WIDTH), lambda i, j: (0, 0))],
        out_specs=pl.BlockSpec((1, tt, B_WIDTH), lambda i, j: (i, j, 0)),
        scratch_shapes=[pltpu.VMEM((tt + POOL_HALO, B_WIDTH), F32)],
        compiler_params=_params("arbitrary", "arbitrary"),
        name="pool",
    )(u, prev16, wbd, scale)


CONV_HALO = SUBLANES_V7X


def _delta_kernel(qkv_ref, z_ref, misc_ref, cprev_ref, h0_ref, cw_ref, alog_ref, dtb_ref, an_ref,
                  seg_ref, selb_ref, selg_ref, o_ref, hout_ref, cb_ref, hs_ref, *, chunk):
    t = pl.program_id(1)
    nt = pl.num_programs(1)
    tc = qkv_ref.shape[1]
    c = chunk
    cast = (lambda a: a.astype(BF16)) if c % 16 == 0 else (lambda a: a)

    def mm(a, b):
        return jnp.dot(cast(a), cast(b), preferred_element_type=F32)

    def mm_nt(a, b):
        return lax.dot_general(cast(a), cast(b), NT_DIMS, preferred_element_type=F32)

    @pl.when(t == 0)
    def _():
        cb_ref[0:CONV_HALO, :] = cprev_ref[0]
        hs_ref[...] = h0_ref[0]

    cb_ref[CONV_HALO:CONV_HALO + tc, :] = qkv_ref[0]
    base = CONV_HALO - (CONV_W - 1)
    y = cb_ref[base:base + tc, :] * cw_ref[0:1, :]
    for j in range(1, CONV_W):
        y = y + cb_ref[base + j:base + j + tc, :] * cw_ref[j:j + 1, :]
    tail = cb_ref[tc:tc + CONV_HALO, :]
    cb_ref[0:CONV_HALO, :] = tail
    y = y * _sigmoid(y)
    q, k, v = y[:, 0:A_WIDTH], y[:, A_WIDTH:2 * A_WIDTH], y[:, 2 * A_WIDTH:]
    split = c % 16 == 0

    def pieces(a, n):
        out, r = [], a
        for _ in range(n - 1):
            p = r.astype(BF16)
            out.append(p)
            r = r - p.astype(F32)
        return out + [r.astype(BF16)]

    def bdot(a, b):
        return jnp.dot(a, b, preferred_element_type=F32)

    def dot_sel(a, sel01, n=3):
        if not split:
            return _hdot(a, sel01)
        b = sel01.astype(BF16)
        terms = [bdot(p, b) for p in pieces(a, n)]
        return functools.reduce(lambda x, y: y + x, reversed(terms))

    def sel_dot(sel01, b):
        if not split:
            return _hdot(sel01, b)
        b1, b2, b3 = pieces(b, 3)
        a = sel01.astype(BF16)
        return bdot(a, b1) + (bdot(a, b2) + bdot(a, b3))

    def dot_hl(a, b):
        if not split:
            return _hdot(a, b)
        a1, a2 = pieces(a, 2)
        b1, b2 = pieces(b, 2)
        return bdot(a1, b1) + (bdot(a1, b2) + bdot(a2, b1))

    seg = seg_ref[...]
    qn = q * lax.rsqrt(dot_sel(q * q, seg, 2) + 1e-6) * (HEAD_DIM ** -0.5)
    kn = k * lax.rsqrt(dot_sel(k * k, seg, 2) + 1e-6)
    m = misc_ref[0]
    beta_f = dot_sel(_sigmoid(m), selb_ref[...], 2)
    g_f = dot_sel(-jnp.exp(alog_ref[...]) * _softplus(m + dtb_ref[...]), selg_ref[...])
    z = z_ref[0]

    ri = lax.broadcasted_iota(I32, (c, c), 0)
    ci = lax.broadcasted_iota(I32, (c, c), 1)
    incl = ri >= ci
    strict = ri > ci
    eye = (ri == ci).astype(F32)
    ltri = incl.astype(F32)
    ones_cc = jnp.ones((c, c), F32)
    r64 = lax.broadcasted_iota(I32, (HEAD_DIM, HEAD_DIM), 0)
    c64 = lax.broadcasted_iota(I32, (HEAD_DIM, HEAD_DIM), 1)
    eye64 = (r64 == c64).astype(F32)
    levels = max(int(math.log2(c)) - 1, 0)

    nck = tc // c
    heads = range(A_HEADS)
    rows = [slice(ck * c, (ck + 1) * c) for ck in range(nck)]
    lanes = [slice(h * HEAD_DIM, (h + 1) * HEAD_DIM) for h in heads]
    pairs = [(ck, h) for ck in range(nck) for h in heads]

    gc = [sel_dot(ltri, g_f[r]) for r in rows]
    eg = [jnp.exp(g) for g in gc]
    glast = [g[c - 1:c, :] for g in gc]
    kb = [kn[r] * beta_f[r] for r in rows]
    vb = [v[r] * beta_f[r] for r in rows]
    kbg = [kb[ck] * eg[ck] for ck in range(nck)]
    qg = [qn[r] * eg[ck] for ck, r in enumerate(rows)]
    kdec = [kn[r] * jnp.exp(glast[ck] - gc[ck]) for ck, r in enumerate(rows)]
    hdec = [jnp.exp(g) for g in glast]

    if c == HEAD_DIM:
        rt = lax.broadcasted_iota(I32, (c, A_WIDTH), 0)
        ct = lax.broadcasted_iota(I32, (c, A_WIDTH), 1) % HEAD_DIM
        incl_t = rt >= ct
        eye_t = (rt == ct).astype(F32)
        decay = {}
        for ck in range(nck):
            grow = sel_dot(ones_cc, gc[ck] * eye_t)
            dall = jnp.where(incl_t, jnp.exp(jnp.where(incl_t, gc[ck] - grow, 0.0)), 0.0)
            for h in heads:
                decay[ck, h] = dall[:, lanes[h]]
    else:
        decay = {}
        for ck, h in pairs:
            gcol = gc[ck][:, h * HEAD_DIM:h * HEAD_DIM + c]
            grow = sel_dot(ones_cc, gcol * eye)
            decay[ck, h] = jnp.where(incl, jnp.exp(jnp.where(incl, gcol - grow, 0.0)), 0.0)

    kk = {(ck, h): mm_nt(kb[ck][:, lanes[h]], kn[rows[ck], lanes[h]]) for ck, h in pairs}
    qk = {(ck, h): mm_nt(qn[rows[ck], lanes[h]], kn[rows[ck], lanes[h]]) for ck, h in pairs}
    kdec_t = {(ck, h): mm_nt(eye64, kdec[ck][:, lanes[h]]) for ck, h in pairs}
    lm = {p: jnp.where(strict, kk[p] * decay[p], 0.0) for p in pairs}
    intra = {p: jnp.where(incl, qk[p] * decay[p], 0.0) for p in pairs}
    tinv = {p: eye - lm[p] for p in pairs}
    pw = lm
    for _ in range(levels):
        pw = {p: dot_hl(pw[p], pw[p]) for p in pairs}
        tinv = {p: tinv[p] + dot_hl(tinv[p], pw[p]) for p in pairs}
    u = {(ck, h): dot_hl(tinv[ck, h], vb[ck][:, lanes[h]]) for ck, h in pairs}
    w = {(ck, h): dot_hl(tinv[ck, h], kbg[ck][:, lanes[h]]) for ck, h in pairs}

    for ck in range(nck):
        hst = [hs_ref[h] for h in heads]
        v_new = [u[ck, h] - mm(w[ck, h], hst[h]) for h in heads]
        o_st = [mm(qg[ck][:, lanes[h]], hst[h]) for h in heads]
        o = [o_st[h] + mm(intra[ck, h], v_new[h]) for h in heads]
        for h in heads:
            hs_ref[h] = hst[h] * hdec[ck][:, lanes[h]] + mm(kdec_t[ck, h], v_new[h])
        outs = []
        for h in heads:
            ms = jnp.mean(o[h] * o[h], axis=-1, keepdims=True)
            zh = z[rows[ck], lanes[h]]
            outs.append(o[h] * lax.rsqrt(ms + EPS) * an_ref[...] * (zh * _sigmoid(zh)))
        o_ref[0, rows[ck], :] = jnp.concatenate(outs, axis=1)

    @pl.when(t == nt - 1)
    def _():
        hout_ref[0] = hs_ref[...]


def _delta(qkv, z, misc, cprev8, h0, cw8, alog, dtb, an, consts, tc, chunk):
    b, t, _ = qkv.shape
    assert t % tc == 0 and tc % chunk == 0
    seg, selb, selg = consts
    bt = lambda width: pl.BlockSpec((1, tc, width), lambda i, j: (i, j, 0))
    const = lambda shape: pl.BlockSpec(shape, lambda i, j: tuple(0 for _ in shape))
    hspec = pl.BlockSpec((1, A_HEADS, HEAD_DIM, HEAD_DIM), lambda i, j: (i, 0, 0, 0))
    return pl.pallas_call(
        functools.partial(_delta_kernel, chunk=chunk),
        out_shape=(jax.ShapeDtypeStruct((b, t, A_WIDTH), F32),
                   jax.ShapeDtypeStruct((b, A_HEADS, HEAD_DIM, HEAD_DIM), F32)),
        grid=(b, t // tc),
        in_specs=[bt(3 * A_WIDTH), bt(A_WIDTH), bt(MISC_W),
                  pl.BlockSpec((1, CONV_HALO, 3 * A_WIDTH), lambda i, j: (i, 0, 0)),
                  hspec, const((CONV_HALO, 3 * A_WIDTH)), const((1, MISC_W)), const((1, MISC_W)),
                  const((1, HEAD_DIM)), const((A_WIDTH, A_WIDTH)), const((MISC_W, A_WIDTH)),
                  const((MISC_W, A_WIDTH))],
        out_specs=(bt(A_WIDTH), hspec),
        scratch_shapes=[pltpu.VMEM((tc + CONV_HALO, 3 * A_WIDTH), F32),
                        pltpu.VMEM((A_HEADS, HEAD_DIM, HEAD_DIM), F32)],
        compiler_params=_params("arbitrary", "arbitrary"),
        name="delta",
    )(qkv, z, misc, cprev8, h0, cw8, alog, dtb, an, seg, selb, selg)


PROMPT_QUERY_TILE = 2 * LANES_V7X
KEY_BLOCK = 512
KEY_STEP = 128
COUNT_ROWS = 4 * SUBLANES_V7X
BIT_GROUP_ROWS = 32 * SUBLANES_V7X


def _transpose32(words):
    a = list(words)
    j, mask = 16, 0x0000FFFF
    while j:
        k = 0
        while k < 32:
            t = (a[k] ^ (a[k + j] >> j)) & mask
            a[k] = a[k] ^ t
            a[k + j] = a[k + j] ^ (t << j)
            k = (k + j + 1) & ~j
        j >>= 1
        mask ^= (mask << j) & 0xFFFFFFFF
    return a


def _dsa_prompt_kernel(q_ref, qi_ref, miscq_ref, ki_ref, kt_ref, vt_ref, o_ref,
                       skey_ref, jc_ref, planes_ref, active_ref, qs_ref, qext_ref, bias_ref, s_ref, p_ref, m_ref, a_ref, l_ref, acc_ref,
                       *, k_sel, idx_bits, kb):
    i = pl.program_id(1)
    tq = q_ref.shape[1]
    ks = KEY_STEP
    nsub = kb // ks
    nkb = lax.div((i + 1) * tq + (kb - 1), kb)

    wt = miscq_ref[0].T[MISC_WI:MISC_WI + IDX_HEADS, :] * (IDX_DIM ** -0.5 * IDX_HEADS ** -0.5)
    qs_ref[...] = qi_ref[0].T.astype(BF16)
    qpos = i * tq + lax.broadcasted_iota(I32, (ks, tq), 1)
    krow = lax.broadcasted_iota(I32, (ks, tq), 0)

    def score_block(jb, carry):
        for u in range(nsub):
            rows = slice(u * ks, (u + 1) * ks)
            kit = ki_ref[0, jb, rows, :]
            ds = [jnp.dot(kit, qs_ref[h * IDX_DIM:(h + 1) * IDX_DIM, :], preferred_element_type=F32)
                  for h in range(IDX_HEADS)]
            s = jnp.maximum(ds[0], 0.0) * wt[0:1, :]
            for h in range(1, IDX_HEADS):
                s = s + jnp.maximum(ds[h], 0.0) * wt[h:h + 1, :]
            skey_ref[jb, rows, :] = _score_key(s)
        return carry

    lax.fori_loop(0, nkb, score_block, 0)
    last = nkb - 1
    for u in range(nsub):
        rows = slice(u * ks, (u + 1) * ks)
        adm = last * kb + u * ks + krow <= qpos
        skey_ref[last, rows, :] = jnp.where(adm, skey_ref[last, rows, :], NEG_INF_KEY)

    def count(pred):
        def body(jb, acc):
            hit = pred(skey_ref[jb], jb).astype(I32).reshape(kb // COUNT_ROWS, COUNT_ROWS, tq)
            return acc + jnp.sum(hit, axis=0)
        acc = lax.fori_loop(0, nkb, body, jnp.zeros((COUNT_ROWS, tq), I32))
        return jnp.sum(acc, axis=0, keepdims=True)

    @pl.when((i == 0) & (pl.program_id(0) == 0))
    def _():
        planes_ref[...] = jnp.zeros(planes_ref.shape, I32)

    gpb = kb // BIT_GROUP_ROWS
    n_groups = planes_ref.shape[1]

    def plane_block(jb, carry):
        for gsub in range(gpb):
            for c0 in range(0, tq, LANES_V7X):
                cols = slice(c0, c0 + LANES_V7X)
                words = [skey_ref[jb, gsub * BIT_GROUP_ROWS + SUBLANES_V7X * k:
                                  gsub * BIT_GROUP_ROWS + SUBLANES_V7X * (k + 1), cols] for k in range(32)]
                for w, word in enumerate(_transpose32(words)):
                    planes_ref[31 - w, jb * gpb + gsub, :, cols] = word
        return carry

    lax.fori_loop(0, nkb, plane_block, 0)

    ng = nkb * gpb
    for g in range(n_groups):
        active_ref[g] = jnp.zeros((SUBLANES_V7X, tq), I32) + jnp.where(g < ng, -1, 0)

    def plane(bit, g):
        return ~planes_ref[bit, g] if bit == 31 else planes_ref[bit, g]

    above = jnp.zeros((1, tq), I32)
    m_bits = jnp.zeros((1, tq), I32)
    keep = None
    for bit in range(31, -2, -1):
        parts = [jnp.zeros((SUBLANES_V7X, tq), I32) for _ in range(4)]
        for g in range(n_groups):
            act = active_ref[g]
            if keep is not None:
                act = act & ~(plane(bit + 1, g) ^ keep)
                active_ref[g] = act
            hit = act & plane(bit, g) if bit >= 0 else act
            parts[g % 4] = parts[g % 4] + lax.population_count(hit)
        cnt = jnp.sum((parts[0] + parts[1]) + (parts[2] + parts[3]), axis=0, keepdims=True)
        if bit < 0:
            n_ge = above + cnt
            break
        take = above + cnt >= k_sel
        above = jnp.where(take, above, above + cnt)
        m_bits = m_bits | jnp.where(take, jnp.int32(INT_MIN if bit == 31 else 1 << bit), 0)
        keep = jnp.broadcast_to(jnp.where(take, -1, 0), (SUBLANES_V7X, tq))
    m = m_bits ^ INT_MIN

    jc_ref[...] = jnp.full(jc_ref.shape, (1 << idx_bits) - 1, I32)

    @pl.when(jnp.max(n_ge) > k_sel)
    def _():
        need = k_sel - above
        brow = lax.broadcasted_iota(I32, (kb, tq), 0)
        jcut = jnp.zeros((1, tq), I32)
        for bit in range(idx_bits - 1, -1, -1):
            cand = jcut | (1 << bit)
            below = count(lambda blk, jb: jnp.where(blk == m, jb * kb + brow, BIG_IDX) < cand)
            jcut = jnp.where(below < need, cand, jcut)
        jc_ref[...] = jnp.broadcast_to(jcut, jc_ref.shape)

    jcut = jc_ref[0:1, :]

    q_t = (q_ref[0] * (HEAD_DIM ** -0.5 * math.log2(math.e))).T
    pair_row = lax.broadcasted_iota(I32, (LANES_V7X, tq), 0) // HEAD_DIM
    for h in range(C_HEADS):
        qg = q_t[(h // 2) * LANES_V7X:(h // 2 + 1) * LANES_V7X, :]
        qext_ref[h] = jnp.where(pair_row == h % 2, qg, 0.0).astype(BF16)
    m_ref[...] = jnp.full(m_ref.shape, MASKED, F32)
    l_ref[...] = jnp.zeros(l_ref.shape, F32)
    acc_ref[...] = jnp.zeros(acc_ref.shape, F32)
    sgrp = ks // SUBLANES_V7X
    dyn0 = pl.multiple_of(lax.shift_right_logical(i, 24) * ks, ks)

    def att_block(jb, carry):
        for u in range(nsub):
            rows = slice(u * ks, (u + 1) * ks)
            kidx = jb * kb + u * ks + krow
            sel = jnp.where(kidx <= qpos, _cut_rank(skey_ref[jb, rows, :], m, kidx), BIG_IDX) <= jcut
            bias_ref[rows, :] = jnp.where(sel, 0.0, MASKED)
        tops = [None] * C_HEADS
        for u in range(nsub):
            rows = slice(u * ks, (u + 1) * ks)
            for h in range(C_HEADS):
                g = h // 2
                kt = kt_ref[0, jb, rows, g * LANES_V7X:(g + 1) * LANES_V7X]
                s = bias_ref[rows, :] + jnp.dot(kt, qext_ref[h], preferred_element_type=F32)
                s_ref[h, rows, :] = s
                top = jnp.max(s.reshape(sgrp, SUBLANES_V7X, tq), axis=0)
                tops[h] = top if u == 0 else jnp.maximum(tops[h], top)
        for h in range(C_HEADS):
            m_old = m_ref[h]
            m_new = jnp.maximum(m_old, jnp.max(tops[h], axis=0, keepdims=True))
            m_ref[h] = m_new
            a_ref[h] = jnp.exp2(m_old - m_new)
        for h in range(C_HEADS):
            m_new = m_ref[h]
            part = a_ref[h] * l_ref[h]
            for u in range(nsub):
                rows = slice(u * ks, (u + 1) * ks)
                s = s_ref[h, pl.ds(dyn0 + u * ks, ks), :]
                p = jnp.exp2(s.reshape(sgrp, SUBLANES_V7X, tq) - m_new[None])
                part = part + jnp.sum(p, axis=0)
                p_ref[h, rows, :] = p.reshape(ks, tq).astype(BF16)
            l_ref[h] = part
        for h in range(C_HEADS):
            vt = vt_ref[0, jb, h * HEAD_DIM:(h + 1) * HEAD_DIM, :]
            p = p_ref[h, pl.ds(dyn0, kb), :]
            acc_ref[h] = a_ref[h][0:1, :] * acc_ref[h] + jnp.dot(vt, p, preferred_element_type=F32)
        return carry

    lax.fori_loop(0, nkb, att_block, 0)
    o_t = jnp.concatenate([acc_ref[h] / jnp.sum(l_ref[h], axis=0, keepdims=True) for h in range(C_HEADS)],
                          axis=0)
    o_ref[0] = o_t.T


def _dsa_prompt(q, k, v, qi, misc, tq):
    b, t, _ = q.shape
    kb = min(KEY_BLOCK, t)
    assert t % tq == 0 and tq % LANES_V7X == 0 and t % kb == 0 and kb % KEY_STEP == 0
    nb = t // kb
    k_sel = min(TOPK_MAX, t // 4)
    idx_bits = max(1, (t - 1).bit_length())
    ki = misc[:, :, MISC_KI:MISC_KI + IDX_DIM].astype(BF16).reshape(b, nb, kb, IDX_DIM)
    kt = k.reshape(b, nb, kb, C_WIDTH).astype(BF16)
    vt = jnp.swapaxes(v.reshape(b, nb, kb, C_WIDTH), 2, 3).astype(BF16)
    qblk = lambda width: pl.BlockSpec((1, tq, width), lambda i, j: (i, j, 0))
    seq = lambda d2, d3: pl.BlockSpec((1, nb, d2, d3), lambda i, j: (i, 0, 0, 0))
    return pl.pallas_call(
        functools.partial(_dsa_prompt_kernel, k_sel=k_sel, idx_bits=idx_bits, kb=kb),
        out_shape=jax.ShapeDtypeStruct((b, t, C_WIDTH), F32),
        grid=(b, t // tq),
        in_specs=[qblk(C_WIDTH), qblk(QI_WIDTH), qblk(MISC_W),
                  seq(kb, IDX_DIM), seq(kb, C_WIDTH), seq(C_WIDTH, kb)],
        out_specs=qblk(C_WIDTH),
        scratch_shapes=[pltpu.VMEM((nb, kb, tq), I32), pltpu.VMEM((SUBLANES_V7X, tq), I32),
                        pltpu.VMEM((32, t // BIT_GROUP_ROWS, SUBLANES_V7X, tq), I32),
                        pltpu.VMEM((t // BIT_GROUP_ROWS, SUBLANES_V7X, tq), I32),
                        pltpu.VMEM((QI_WIDTH, tq), BF16),
                        pltpu.VMEM((C_HEADS, LANES_V7X, tq), BF16),
                        pltpu.VMEM((kb, tq), F32),
                        pltpu.VMEM((C_HEADS, kb, tq), F32),
                        pltpu.VMEM((C_HEADS, kb, tq), BF16),
                        pltpu.VMEM((C_HEADS, SUBLANES_V7X, tq), F32),
                        pltpu.VMEM((C_HEADS, SUBLANES_V7X, tq), F32),
                        pltpu.VMEM((C_HEADS, SUBLANES_V7X, tq), F32),
                        pltpu.VMEM((C_HEADS, HEAD_DIM, tq), F32)],
        compiler_params=_params("arbitrary", "arbitrary"),
        name="dsa_prompt",
    )(q, qi, misc, ki, kt, vt)


SAMPLE_PAGES_PER_STEP = 8


def _dsa_sample_index_kernel(pt_ref, *refs, k_sel, n_steps, pps, idx_bits):
    kidx_refs = refs[:pps]
    qi_ref, misc_ref, skp_ref, skn_ref, m_ref, jc_ref, sk_ref = refs[pps:]
    j = pl.program_id(1)
    t = qi_ref.shape[1]
    misc = misc_ref[0]
    qi = qi_ref[0]
    qs = jnp.concatenate([qi[:, h * IDX_DIM:(h + 1) * IDX_DIM] for h in range(IDX_HEADS)],
                         axis=0).astype(BF16)
    ws = jnp.concatenate([misc[:, MISC_WI + h:MISC_WI + h + 1] for h in range(IDX_HEADS)], axis=0)
    ws = ws * (IDX_DIM ** -0.5 * IDX_HEADS ** -0.5)
    lane = lax.broadcasted_iota(I32, (t, PAGE_SIZE), 1)
    row = lax.broadcasted_iota(I32, (t, PAGE_SIZE), 0)

    def keys_of(ki_tiles):
        ds = [lax.dot_general(qs, kt.astype(BF16), NT_DIMS, preferred_element_type=F32) for kt in ki_tiles]
        out = []
        for d in ds:
            sw = jnp.maximum(d, 0.0) * ws
            s = sw[0:t]
            for h in range(1, IDX_HEADS):
                s = s + sw[h * t:(h + 1) * t]
            out.append(_score_key(s))
        return out

    @pl.when(j < n_steps)
    def _():
        for r, key in enumerate(keys_of([kidx_refs[r][0, 0] for r in range(pps)])):
            sk_ref[j * pps + r] = key
            skp_ref[0, :, r * PAGE_SIZE:(r + 1) * PAGE_SIZE] = key

    @pl.when(j == n_steps)
    def _():
        ki_new = misc[:, MISC_KI:MISC_KI + IDX_DIM]
        ki_pad = jnp.concatenate([ki_new, jnp.zeros((PAGE_SIZE - t, IDX_DIM), F32)], axis=0)
        key = jnp.where(lane <= row, keys_of([ki_pad])[0], NEG_INF_KEY)
        sk_ref[n_steps * pps] = key
        skn_ref[0] = key
        allk = sk_ref[...]
        idx = (lax.broadcasted_iota(I32, allk.shape, 0) * PAGE_SIZE
               + lax.broadcasted_iota(I32, allk.shape, 2))

        def count(mask):
            return jnp.sum(jnp.sum(mask.astype(I32), axis=0), axis=1, keepdims=True)

        m = jnp.full((t, 1), INT_MIN, I32)
        for bit in range(31, -1, -1):
            cand = (m ^ INT_MIN) if bit == 31 else (m | (1 << bit))
            m = jnp.where(count(allk >= cand[None]) >= k_sel, cand, m)
        need = k_sel - count(allk >= (m + 1)[None])
        tied_idx = jnp.where(allk == m[None], idx, BIG_IDX)
        jcut = jnp.zeros((t, 1), I32)
        for bit in range(idx_bits - 1, -1, -1):
            cand = jcut | (1 << bit)
            below = count(tied_idx < cand[None])
            jcut = jnp.where(below < need, cand, jcut)
        m_ref[0] = jnp.broadcast_to(m, (t, LANES_V7X))
        jc_ref[0] = jnp.broadcast_to(jcut, (t, LANES_V7X))


def _dsa_sample_attend_kernel(pt_ref, *refs, n_steps, pps):
    kc_refs, vc_refs = refs[:pps], refs[pps:2 * pps]
    (skp_ref, skn_ref, m_ref, jc_ref, q_ref, kn_ref, vn_ref,
     o_ref, qbd_ref, ms_ref, ls_ref, acc_ref) = refs[2 * pps:]
    j = pl.program_id(1)
    t = q_ref.shape[1]

    @pl.when(j == 0)
    def _():
        q = q_ref[0] * (HEAD_DIM ** -0.5)
        head = lax.broadcasted_iota(I32, q.shape, 1) // HEAD_DIM
        qbd_ref[...] = jnp.concatenate([jnp.where(head == h, q, 0.0) for h in range(C_HEADS)], axis=0)
        ms_ref[...] = jnp.full(ms_ref.shape, MASKED, F32)
        ls_ref[...] = jnp.zeros(ls_ref.shape, F32)
        acc_ref[...] = jnp.zeros(acc_ref.shape, F32)

    m = m_ref[0]
    jcut6 = jnp.concatenate([jc_ref[0]] * C_HEADS, axis=0)
    lane = lax.broadcasted_iota(I32, m.shape, 1)
    row = lax.broadcasted_iota(I32, m.shape, 0)

    def step(keys, first_idx, last_adm, kps, vps):
        qbd = qbd_ref[...].astype(BF16)
        ss = [lax.dot_general(qbd, kp.astype(BF16), NT_DIMS, preferred_element_type=F32) for kp in kps]
        masked = []
        for r, s in enumerate(ss):
            rank = jnp.where(lane <= last_adm, _cut_rank(keys[r], m, first_idx + r * PAGE_SIZE + lane), BIG_IDX)
            sel = jnp.concatenate([rank] * C_HEADS, axis=0) <= jcut6
            masked.append(jnp.where(sel, s, MASKED))
        top = functools.reduce(jnp.maximum, masked)
        mprev = ms_ref[:, 0:1]
        mn = jnp.maximum(mprev, jnp.max(top, axis=1, keepdims=True))
        alpha = jnp.exp(mprev - mn)
        ps = [jnp.exp(s - mn) for s in masked]
        psum = functools.reduce(lambda a, b: a + b, ps)
        pv = [jnp.dot(p.astype(BF16), vp.astype(BF16), preferred_element_type=F32) for p, vp in zip(ps, vps)]
        ls_ref[...] = jnp.broadcast_to(alpha * ls_ref[:, 0:1] + jnp.sum(psum, axis=1, keepdims=True),
                                       ls_ref.shape)
        acc_ref[...] = alpha * acc_ref[...] + functools.reduce(lambda a, b: a + b, pv)
        ms_ref[...] = jnp.broadcast_to(mn, ms_ref.shape)

    @pl.when(j < n_steps)
    def _():
        keys = [skp_ref[0, :, r * PAGE_SIZE:(r + 1) * PAGE_SIZE] for r in range(pps)]
        step(keys, j * (pps * PAGE_SIZE), PAGE_SIZE,
             [kc_refs[r][0, 0] for r in range(pps)], [vc_refs[r][0, 0] for r in range(pps)])

    @pl.when(j == n_steps)
    def _():
        pad = jnp.zeros((PAGE_SIZE - t, C_WIDTH), F32)
        step([skn_ref[0]], n_steps * pps * PAGE_SIZE, row,
             [jnp.concatenate([kn_ref[0], pad], axis=0)], [jnp.concatenate([vn_ref[0], pad], axis=0)])
        o_all = acc_ref[...] / ls_ref[:, 0:1]
        head = lax.broadcasted_iota(I32, (t, C_WIDTH), 1) // HEAD_DIM
        o = jnp.zeros((t, C_WIDTH), F32)
        for h in range(C_HEADS):
            o = o + jnp.where(head == h, o_all[h * t:(h + 1) * t], 0.0)
        o_ref[0] = o


def _dsa_sample(cache_k, cache_v, cache_kidx, layer, page_table, q, k, v, qi, misc):
    b, t, _ = q.shape
    n_pages = page_table.shape[1]
    past = n_pages * PAGE_SIZE
    k_sel = min(TOPK_MAX, (past + t) // 4)
    idx_bits = max(1, (past + PAGE_SIZE - 1).bit_length())
    pps = math.gcd(n_pages, SAMPLE_PAGES_PER_STEP)
    n_steps = n_pages // pps

    def page(r):
        return lambda i, j, pt: (layer, pt[i, jnp.minimum(j, n_steps - 1) * pps + r], 0, 0)

    pages = lambda wd: [pl.BlockSpec((1, 1, PAGE_SIZE, wd), page(r)) for r in range(pps)]
    per_b = lambda wd: pl.BlockSpec((1, t, wd), lambda i, j, pt: (i, 0, 0))
    skp_spec = pl.BlockSpec((1, t, pps * PAGE_SIZE), lambda i, j, pt: (i, 0, jnp.minimum(j, n_steps - 1)))

    skp, skn, mth, jcut = pl.pallas_call(
        functools.partial(_dsa_sample_index_kernel, k_sel=k_sel, n_steps=n_steps, pps=pps, idx_bits=idx_bits),
        out_shape=(jax.ShapeDtypeStruct((b, t, past), I32),
                   jax.ShapeDtypeStruct((b, t, PAGE_SIZE), I32),
                   jax.ShapeDtypeStruct((b, t, LANES_V7X), I32),
                   jax.ShapeDtypeStruct((b, t, LANES_V7X), I32)),
        grid_spec=pltpu.PrefetchScalarGridSpec(
            num_scalar_prefetch=1,
            grid=(b, n_steps + 1),
            in_specs=pages(IDX_DIM) + [per_b(QI_WIDTH), per_b(MISC_W)],
            out_specs=(skp_spec, per_b(PAGE_SIZE), per_b(LANES_V7X), per_b(LANES_V7X)),
            scratch_shapes=[pltpu.VMEM((n_pages + 1, t, PAGE_SIZE), I32)]),
        compiler_params=_params("arbitrary", "arbitrary"),
        name="dsa_sample_index",
    )(page_table, *([cache_kidx] * pps), qi, misc)

    rows = C_HEADS * t
    return pl.pallas_call(
        functools.partial(_dsa_sample_attend_kernel, n_steps=n_steps, pps=pps),
        out_shape=jax.ShapeDtypeStruct((b, t, C_WIDTH), F32),
        grid_spec=pltpu.PrefetchScalarGridSpec(
            num_scalar_prefetch=1,
            grid=(b, n_steps + 1),
            in_specs=pages(C_WIDTH) + pages(C_WIDTH)
            + [skp_spec, per_b(PAGE_SIZE), per_b(LANES_V7X), per_b(LANES_V7X),
               per_b(C_WIDTH), per_b(C_WIDTH), per_b(C_WIDTH)],
            out_specs=per_b(C_WIDTH),
            scratch_shapes=[pltpu.VMEM((rows, C_WIDTH), F32), pltpu.VMEM((rows, LANES_V7X), F32),
                            pltpu.VMEM((rows, LANES_V7X), F32), pltpu.VMEM((rows, C_WIDTH), F32)]),
        compiler_params=_params("arbitrary", "arbitrary"),
        name="dsa_sample_attend",
    )(page_table, *([cache_k] * pps), *([cache_v] * pps), skp, skn, mth, jcut, q, k, v)


def _prep_w_in(w):
    o = np.cumsum([0, 3 * A_WIDTH, A_WIDTH, A_HEADS, A_HEADS, B_WIDTH, C_WIDTH, C_WIDTH, C_WIDTH,
                   QI_WIDTH, IDX_DIM, IDX_HEADS])
    part = lambda n: w[:, o[n]:o[n + 1]]
    qkv, z, bb, aa, u, q, k, v, qi, ki, wi = (part(n) for n in range(11))
    pad = jnp.zeros((w.shape[0], MISC_W - (IDX_DIM + IDX_HEADS + 2 * A_HEADS)), w.dtype)
    return jnp.concatenate([qkv, z, u, q, k, v, qi, ki, wi, bb, aa, pad], axis=1).astype(BF16)


def _rope_tabs(pos):
    inv = ROPE_THETA ** (-jnp.arange(ROT_HALF, dtype=F32) / ROT_HALF)
    ang = pos.astype(F32)[:, None] * inv[None, :]
    cos, sin = jnp.cos(ang), jnp.sin(ang)
    n = pos.shape[0]
    rest = HEAD_DIM - ROT_DIM
    ct = jnp.concatenate([cos, cos, jnp.ones((n, rest), F32)], axis=1)
    sa = jnp.concatenate([-sin, jnp.zeros((n, rest + ROT_HALF), F32)], axis=1)
    sb = jnp.concatenate([jnp.zeros((n, ROT_HALF), F32), sin, jnp.zeros((n, rest), F32)], axis=1)
    rep = LANES_V7X // HEAD_DIM
    return tuple(jnp.tile(a, (1, rep)) for a in (ct, sa, sb))


def _delta_consts():
    lane = np.arange(A_WIDTH)
    seg = (lane[:, None] // HEAD_DIM == lane[None, :] // HEAD_DIM).astype(np.float32)
    src = np.arange(MISC_W)
    selb = (src[:, None] == MISC_B + lane[None, :] // HEAD_DIM).astype(np.float32)
    selg = (src[:, None] == MISC_A + lane[None, :] // HEAD_DIM).astype(np.float32)
    return jnp.asarray(seg), jnp.asarray(selb), jnp.asarray(selg)


def _misc_lanes(vec, off):
    return jnp.zeros((1, MISC_W), F32).at[0, off:off + vec.shape[0]].set(vec.astype(F32))


def _block_diag(pw):
    n = pw.shape[0]
    out = jnp.zeros((n * POOL_GROUP, n * POOL_GROUP), pw.dtype)
    for gi in range(n):
        out = out.at[gi * POOL_GROUP:(gi + 1) * POOL_GROUP, gi * POOL_GROUP:(gi + 1) * POOL_GROUP].set(pw[gi])
    return out


def _layer(x2d, b, t, pos0, tabs, conv_prev, pool_prev, delta_prev, attend, wts, consts, tiles, final):
    (g1, w_in, conv_w, a_log, dt_bias, a_norm, pool_w, pool_scale, w_out, g2, w_gate, w_up, w_down, gf) = wts
    tm, tc, tt = tiles
    qkv, z, u, q, k, v, qi, misc = _in_proj(x2d, g1, w_in, tabs, t, tm)
    r3 = lambda a: a.reshape(b, t, a.shape[-1])
    qkv, z, u, q, k, v, qi, misc = (r3(a) for a in (qkv, z, u, q, k, v, qi, misc))

    cprev8 = jnp.concatenate([jnp.zeros((b, CONV_HALO - (CONV_W - 1), 3 * A_WIDTH), F32), conv_prev], axis=1)
    o_a, delta_new = _delta(qkv, z, misc, cprev8, delta_prev, conv_w, a_log, dt_bias, a_norm, consts,
                            tc, math.gcd(t, DELTA_CHUNK))
    prev16 = jnp.concatenate([jnp.zeros((b, POOL_HALO - POOL_STATE, B_WIDTH), F32), pool_prev], axis=1)
    o_b = _pool(u, prev16, pool_w, pool_scale, pos0, tt)
    o_c = attend(q, k, v, qi, misc)

    flat = lambda a: a.reshape(b * t, a.shape[-1])
    y = _out_ffn(x2d, flat(o_a), flat(o_b), flat(o_c), w_out, g2, w_gate, w_up, w_down, gf, tm, final)
    conv_new = jnp.concatenate([conv_prev, qkv], axis=1)[:, -(CONV_W - 1):]
    pool_new = jnp.concatenate([pool_prev, u], axis=1)[:, -POOL_STATE:]
    state = (k.reshape(b, t, C_HEADS, HEAD_DIM), v.reshape(b, t, C_HEADS, HEAD_DIM),
             misc[:, :, MISC_KI:MISC_KI + IDX_DIM], conv_new, pool_new, delta_new)
    return y, state


def kernel(x_prompt, x_sample, cache_k, cache_v, cache_kidx, state_conv, state_pool, state_delta,
           page_table, norm1, w_in, conv_w, a_log, dt_bias, a_norm, pool_w, pool_scale,
           w_out, norm2, w_gate, w_up, w_down, norm_f):
    depth = w_in.shape[0]
    bp, tp, _ = x_prompt.shape
    bs, ts, _ = x_sample.shape
    n_pages = page_table.shape[1]
    past = n_pages * PAGE_SIZE
    n_pool = cache_k.shape[1]

    tabs_p = _rope_tabs(jnp.arange(tp, dtype=I32))
    tabs_s = _rope_tabs(past + jnp.arange(ts, dtype=I32))
    consts = _delta_consts()
    ck = cache_k.reshape(depth, n_pool, PAGE_SIZE, C_WIDTH)
    cv = cache_v.reshape(depth, n_pool, PAGE_SIZE, C_WIDTH)
    gf = norm_f.reshape(1, D_MODEL)

    tiles_p = (min(512, bp * tp), min(128, tp), min(512, tp))
    tiles_s = (bs * ts, ts, ts)
    conv0 = jnp.zeros((bp, CONV_W - 1, 3 * A_WIDTH), F32)
    pool0 = jnp.zeros((bp, POOL_STATE, B_WIDTH), F32)
    delta0 = jnp.zeros((bp, A_HEADS, HEAD_DIM, HEAD_DIM), F32)

    hp = x_prompt.reshape(bp * tp, D_MODEL)
    hs = x_sample.reshape(bs * ts, D_MODEL)
    p_st, s_st = [], []
    for l in range(depth):
        cw8 = jnp.concatenate([conv_w[l], jnp.zeros((CONV_HALO - CONV_W, 3 * A_WIDTH), F32)], axis=0)
        wts = (norm1[l].reshape(1, D_MODEL), _prep_w_in(w_in[l]), cw8,
               _misc_lanes(a_log[l], MISC_A), _misc_lanes(dt_bias[l], MISC_A),
               a_norm[l].reshape(1, HEAD_DIM), _block_diag(pool_w[l]).astype(BF16),
               pool_scale[l].reshape(1, B_WIDTH), w_out[l].astype(BF16), norm2[l].reshape(1, D_MODEL),
               w_gate[l].astype(BF16), w_up[l].astype(BF16), w_down[l].astype(BF16), gf)
        final = l == depth - 1
        attend_p = functools.partial(_dsa_prompt, tq=math.gcd(tp, PROMPT_QUERY_TILE))
        hp, sp = _layer(hp, bp, tp, 0, tabs_p, conv0, pool0, delta0, attend_p, wts, consts, tiles_p, final)
        attend_s = functools.partial(_dsa_sample, ck, cv, cache_kidx, l, page_table)
        hs, ss = _layer(hs, bs, ts, past, tabs_s, state_conv[l], state_pool[l], state_delta[l],
                        attend_s, wts, consts, tiles_s, final)
        p_st.append(sp)
        s_st.append(ss)
    outs = [hp.reshape(bp, tp, D_MODEL), hs.reshape(bs, ts, D_MODEL)]
    for sts in (p_st, s_st):
        for n in range(6):
            outs.append(jnp.stack([s[n] for s in sts]))
    return tuple(outs)
```

```python
import functools
import math

import jax
import jax.numpy as jnp
import numpy as np
from jax import lax
from jax.experimental import pallas as pl
from jax.experimental.pallas import tpu as pltpu

F32, BF16, I32 = jnp.float32, jnp.bfloat16, jnp.int32

D_MODEL = 1024
HEAD_DIM = 64
A_HEADS = 6
A_WIDTH = A_HEADS * HEAD_DIM
CONV_W = 4
DELTA_CHUNK = 64
POOL_WINDOWS = (2, 4, 8, 16)
POOL_GROUP = 64
B_WIDTH = POOL_GROUP * len(POOL_WINDOWS)
POOL_STATE = max(POOL_WINDOWS) - 1
C_HEADS = 6
C_WIDTH = C_HEADS * HEAD_DIM
IDX_HEADS = 4
IDX_DIM = 64
QI_WIDTH = IDX_HEADS * IDX_DIM
TOPK_MAX = 256
ROT_DIM = HEAD_DIM // 4
ROT_HALF = ROT_DIM // 2
ROPE_THETA = 500000.0
D_FF = -(-8 * D_MODEL // (3 * 256)) * 256
EPS = 1e-6
PAGE_SIZE = 128

LANES_V7X = 128
SUBLANES_V7X = 8
VMEM_LIMIT_V7X = 56 * 1024 * 1024

MISC_W = LANES_V7X
MISC_KI, MISC_WI, MISC_B, MISC_A = 0, IDX_DIM, IDX_DIM + IDX_HEADS, IDX_DIM + IDX_HEADS + A_HEADS
OFF_QKV = 0
OFF_Z = OFF_QKV + 3 * A_WIDTH
OFF_U = OFF_Z + A_WIDTH
OFF_Q = OFF_U + B_WIDTH
OFF_K = OFF_Q + C_WIDTH
OFF_V = OFF_K + C_WIDTH
OFF_QI = OFF_V + C_WIDTH
OFF_MISC = OFF_QI + QI_WIDTH
IN_COLS_PAD = OFF_MISC + MISC_W

INT_MIN = -(2 ** 31)
NEG_INF_KEY = -2139095041
MASKED = -1e30
NT_DIMS = (((1,), (1,)), ((), ()))
BIG_IDX = 2 ** 30

_HI = lax.Precision.HIGHEST


def _hdot(a, b):
    return jnp.dot(a, b, precision=_HI, preferred_element_type=F32)


def _sigmoid(x):
    return 1.0 / (1.0 + jnp.exp(-x))


def _softplus(x):
    return jnp.maximum(x, 0.0) + jnp.log1p(jnp.exp(-jnp.abs(x)))


def _score_key(s):
    s = jnp.where(s == 0.0, 0.0, s)
    bits = lax.bitcast_convert_type(s, I32)
    return bits ^ ((bits >> 31) & 0x7FFFFFFF)


def _cut_rank(key, m, kidx):
    return jnp.where(key == m, kidx, jnp.where(key > m, -1, BIG_IDX))


def _params(*sem):
    return pltpu.CompilerParams(dimension_semantics=sem, vmem_limit_bytes=VMEM_LIMIT_V7X)


def _rope(y, c, sa, sb):
    w = y.shape[-1]
    return y * c + pltpu.roll(y, w - ROT_HALF, 1) * sa + pltpu.roll(y, ROT_HALF, 1) * sb


def _in_proj_kernel(x_ref, g_ref, w_ref, ct_ref, sa_ref, sb_ref,
                    qkv_ref, z_ref, u_ref, q_ref, k_ref, v_ref, qi_ref, misc_ref):
    x = x_ref[...]
    ms = jnp.mean(x * x, axis=-1, keepdims=True)
    hn = (x * lax.rsqrt(ms + EPS) * g_ref[...]).astype(BF16)

    def proj(a, b):
        return jnp.dot(hn, w_ref[:, a:b], preferred_element_type=F32)

    qkv_ref[...] = proj(OFF_QKV, OFF_Z)
    z_ref[...] = proj(OFF_Z, OFF_U)
    u_ref[...] = proj(OFF_U, OFF_Q)
    v_ref[...] = proj(OFF_V, OFF_QI)
    ct, sa, sb = ct_ref[...], sa_ref[...], sb_ref[...]

    def tiled(t, n):
        return jnp.concatenate([t] * n, axis=1)

    n3 = C_WIDTH // LANES_V7X
    q_ref[...] = _rope(proj(OFF_Q, OFF_K), tiled(ct, n3), tiled(sa, n3), tiled(sb, n3))
    k_ref[...] = _rope(proj(OFF_K, OFF_V), tiled(ct, n3), tiled(sa, n3), tiled(sb, n3))
    n2 = QI_WIDTH // LANES_V7X
    qi_ref[...] = _rope(proj(OFF_QI, OFF_MISC), tiled(ct, n2), tiled(sa, n2), tiled(sb, n2))
    is_ki = lax.broadcasted_iota(I32, ct.shape, 1) < IDX_DIM
    misc_ref[...] = _rope(proj(OFF_MISC, IN_COLS_PAD), jnp.where(is_ki, ct, 1.0),
                          jnp.where(is_ki, sa, 0.0), jnp.where(is_ki, sb, 0.0))


def _in_proj(x2d, g, w, tabs, seq_len, tm):
    n = x2d.shape[0]
    assert n % tm == 0
    if seq_len % tm == 0:
        per = seq_len // tm
        tab_map = lambda i: (i % per, 0)
    else:
        assert tm % seq_len == 0 and n == tm
        tabs = tuple(jnp.tile(t, (tm // seq_len, 1)) for t in tabs)
        tab_map = lambda i: (0, 0)
    row = lambda width: pl.BlockSpec((tm, width), lambda i: (i, 0))
    const = lambda shape: pl.BlockSpec(shape, lambda i: (0, 0))
    widths = (3 * A_WIDTH, A_WIDTH, B_WIDTH, C_WIDTH, C_WIDTH, C_WIDTH, QI_WIDTH, MISC_W)
    return pl.pallas_call(
        _in_proj_kernel,
        out_shape=tuple(jax.ShapeDtypeStruct((n, wd), F32) for wd in widths),
        grid=(n // tm,),
        in_specs=[row(D_MODEL), const((1, D_MODEL)), const((D_MODEL, IN_COLS_PAD))]
        + [pl.BlockSpec((tm, LANES_V7X), tab_map)] * 3,
        out_specs=tuple(row(wd) for wd in widths),
        compiler_params=_params("arbitrary"),
        name="in_proj",
    )(x2d, g, w, *tabs)


FFN_CHUNK = 256


def _out_ffn_kernel(x_ref, oa_ref, ob_ref, oc_ref, wo_ref, g2_ref, wg_ref, wu_ref, wd_ref, gf_ref,
                    y_ref, *, final):
    def mm(a, b):
        return jnp.dot(a, b, preferred_element_type=F32)

    mix = (mm(oa_ref[...].astype(BF16), wo_ref[0:A_WIDTH, :])
           + mm(ob_ref[...].astype(BF16), wo_ref[A_WIDTH:A_WIDTH + B_WIDTH, :])
           + mm(oc_ref[...].astype(BF16), wo_ref[A_WIDTH + B_WIDTH:, :]))
    x1 = x_ref[...] + mix
    ms = jnp.mean(x1 * x1, axis=-1, keepdims=True)
    hn = (x1 * lax.rsqrt(ms + EPS) * g2_ref[...]).astype(BF16)
    acc = x1
    for c0 in range(0, D_FF, FFN_CHUNK):
        gate = mm(hn, wg_ref[:, c0:c0 + FFN_CHUNK])
        up = mm(hn, wu_ref[:, c0:c0 + FFN_CHUNK])
        hid = (gate * _sigmoid(gate) * up).astype(BF16)
        acc = acc + mm(hid, wd_ref[c0:c0 + FFN_CHUNK, :])
    if final:
        ms2 = jnp.mean(acc * acc, axis=-1, keepdims=True)
        acc = acc * lax.rsqrt(ms2 + EPS) * gf_ref[...]
    y_ref[...] = acc


def _out_ffn(x2d, oa, ob, oc, wo, g2, wg, wu, wd, gf, tm, final):
    n = x2d.shape[0]
    assert n % tm == 0 and D_FF % FFN_CHUNK == 0
    row = lambda width: pl.BlockSpec((tm, width), lambda i: (i, 0))
    const = lambda shape: pl.BlockSpec(shape, lambda i: (0, 0))
    return pl.pallas_call(
        functools.partial(_out_ffn_kernel, final=final),
        out_shape=jax.ShapeDtypeStruct((n, D_MODEL), F32),
        grid=(n // tm,),
        in_specs=[row(D_MODEL), row(A_WIDTH), row(B_WIDTH), row(C_WIDTH),
                  const((D_MODEL, D_MODEL)), const((1, D_MODEL)),
                  const((D_MODEL, D_FF)), const((D_MODEL, D_FF)), const((D_FF, D_MODEL)),
                  const((1, D_MODEL))],
        out_specs=row(D_MODEL),
        compiler_params=_params("arbitrary"),
        name="out_ffn",
    )(x2d, oa, ob, oc, wo, g2, wg, wu, wd, gf)


POOL_HALO = max(POOL_WINDOWS)


def _pool_kernel(u_ref, prev_ref, w_ref, sc_ref, o_ref, xb_ref, *, pos0):
    t = pl.program_id(1)
    tt = u_ref.shape[1]

    @pl.when(t == 0)
    def _():
        xb_ref[0:POOL_HALO, :] = prev_ref[0]

    u = u_ref[0]
    xb_ref[POOL_HALO:POOL_HALO + tt, :] = u
    sums = {}
    s = u
    for j in range(1, POOL_HALO):
        s = s + xb_ref[POOL_HALO - j:POOL_HALO - j + tt, :]
        if j + 1 in POOL_WINDOWS:
            sums[j + 1] = s
    halo = xb_ref[tt:tt + POOL_HALO, :]
    xb_ref[0:POOL_HALO, :] = halo
    pos = pos0 + t * tt + lax.broadcasted_iota(I32, (tt, B_WIDTH), 0)
    group = lax.broadcasted_iota(I32, (tt, B_WIDTH), 1) // POOL_GROUP
    mean = jnp.zeros((tt, B_WIDTH), F32)
    for gi, win in enumerate(POOL_WINDOWS):
        cnt = jnp.minimum(win, pos + 1).astype(F32)
        mean = jnp.where(group == gi, sums[win] / cnt, mean)
    d = (mean - u).astype(BF16)
    o_ref[0] = jnp.dot(d, w_ref[...], preferred_element_type=F32) * sc_ref[...]


def _pool(u, prev16, wbd, scale, pos0, tt):
    b, t, _ = u.shape
    assert t % tt == 0
    return pl.pallas_call(
        functools.partial(_pool_kernel, pos0=pos0),
        out_shape=jax.ShapeDtypeStruct((b, t, B_WIDTH), F32),
        grid=(b, t // tt),
        in_specs=[pl.BlockSpec((1, tt, B_WIDTH), lambda i, j: (i, j, 0)),
                  pl.BlockSpec((1, POOL_HALO, B_WIDTH), lambda i, j: (i, 0, 0)),
                  pl.BlockSpec((B_WIDTH, B_WIDTH), lambda i, j: (0, 0)),
                  pl.BlockSpec((1, B_WIDTH), lambda i, j: (0, 0))],
        out_specs=pl.BlockSpec((1, tt, B_WIDTH), lambda i, j: (i, j, 0)),
        scratch_shapes=[pltpu.VMEM((tt + POOL_HALO, B_WIDTH), F32)],
        compiler_params=_params("arbitrary", "arbitrary"),
        name="pool",
    )(u, prev16, wbd, scale)


CONV_HALO = SUBLANES_V7X


def _delta_kernel(qkv_ref, z_ref, misc_ref, cprev_ref, h0_ref, cw_ref, alog_ref, dtb_ref, an_ref,
                  seg_ref, selb_ref, selg_ref, o_ref, hout_ref, cb_ref, hs_ref, *, chunk):
    t = pl.program_id(1)
    nt = pl.num_programs(1)
    tc = qkv_ref.shape[1]
    c = chunk
    cast = (lambda a: a.astype(BF16)) if c % 16 == 0 else (lambda a: a)

    def mm(a, b):
        return jnp.dot(cast(a), cast(b), preferred_element_type=F32)

    def mm_nt(a, b):
        return lax.dot_general(cast(a), cast(b), NT_DIMS, preferred_element_type=F32)

    @pl.when(t == 0)
    def _():
        cb_ref[0:CONV_HALO, :] = cprev_ref[0]
        hs_ref[...] = h0_ref[0]

    cb_ref[CONV_HALO:CONV_HALO + tc, :] = qkv_ref[0]
    base = CONV_HALO - (CONV_W - 1)
    y = cb_ref[base:base + tc, :] * cw_ref[0:1, :]
    for j in range(1, CONV_W):
        y = y + cb_ref[base + j:base + j + tc, :] * cw_ref[j:j + 1, :]
    tail = cb_ref[tc:tc + CONV_HALO, :]
    cb_ref[0:CONV_HALO, :] = tail
    y = y * _sigmoid(y)
    q, k, v = y[:, 0:A_WIDTH], y[:, A_WIDTH:2 * A_WIDTH], y[:, 2 * A_WIDTH:]
    split = c % 16 == 0

    def pieces(a, n):
        out, r = [], a
        for _ in range(n - 1):
            p = r.astype(BF16)
            out.append(p)
            r = r - p.astype(F32)
        return out + [r.astype(BF16)]

    def bdot(a, b):
        return jnp.dot(a, b, preferred_element_type=F32)

    def dot_sel(a, sel01, n=3):
        if not split:
            return _hdot(a, sel01)
        b = sel01.astype(BF16)
        terms = [bdot(p, b) for p in pieces(a, n)]
        return functools.reduce(lambda x, y: y + x, reversed(terms))

    def sel_dot(sel01, b):
        if not split:
            return _hdot(sel01, b)
        b1, b2, b3 = pieces(b, 3)
        a = sel01.astype(BF16)
        return bdot(a, b1) + (bdot(a, b2) + bdot(a, b3))

    def dot_hl(a, b):
        if not split:
            return _hdot(a, b)
        a1, a2 = pieces(a, 2)
        b1, b2 = pieces(b, 2)
        return bdot(a1, b1) + (bdot(a1, b2) + bdot(a2, b1))

    seg = seg_ref[...]
    qn = q * lax.rsqrt(dot_sel(q * q, seg, 2) + 1e-6) * (HEAD_DIM ** -0.5)
    kn = k * lax.rsqrt(dot_sel(k * k, seg, 2) + 1e-6)
    m = misc_ref[0]
    beta_f = dot_sel(_sigmoid(m), selb_ref[...], 2)
    g_f = dot_sel(-jnp.exp(alog_ref[...]) * _softplus(m + dtb_ref[...]), selg_ref[...])
    z = z_ref[0]

    ri = lax.broadcasted_iota(I32, (c, c), 0)
    ci = lax.broadcasted_iota(I32, (c, c), 1)
    incl = ri >= ci
    strict = ri > ci
    eye = (ri == ci).astype(F32)
    ltri = incl.astype(F32)
    ones_cc = jnp.ones((c, c), F32)
    r64 = lax.broadcasted_iota(I32, (HEAD_DIM, HEAD_DIM), 0)
    c64 = lax.broadcasted_iota(I32, (HEAD_DIM, HEAD_DIM), 1)
    eye64 = (r64 == c64).astype(F32)
    levels = max(int(math.log2(c)) - 1, 0)

    nck = tc // c
    heads = range(A_HEADS)
    rows = [slice(ck * c, (ck + 1) * c) for ck in range(nck)]
    lanes = [slice(h * HEAD_DIM, (h + 1) * HEAD_DIM) for h in heads]
    pairs = [(ck, h) for ck in range(nck) for h in heads]

    gc = [sel_dot(ltri, g_f[r]) for r in rows]
    eg = [jnp.exp(g) for g in gc]
    glast = [g[c - 1:c, :] for g in gc]
    kb = [kn[r] * beta_f[r] for r in rows]
    vb = [v[r] * beta_f[r] for r in rows]
    kbg = [kb[ck] * eg[ck] for ck in range(nck)]
    qg = [qn[r] * eg[ck] for ck, r in enumerate(rows)]
    kdec = [kn[r] * jnp.exp(glast[ck] - gc[ck]) for ck, r in enumerate(rows)]
    hdec = [jnp.exp(g) for g in glast]

    if c == HEAD_DIM:
        rt = lax.broadcasted_iota(I32, (c, A_WIDTH), 0)
        ct = lax.broadcasted_iota(I32, (c, A_WIDTH), 1) % HEAD_DIM
        incl_t = rt >= ct
        eye_t = (rt == ct).astype(F32)
        decay = {}
        for ck in range(nck):
            grow = sel_dot(ones_cc, gc[ck] * eye_t)
            dall = jnp.where(incl_t, jnp.exp(jnp.where(incl_t, gc[ck] - grow, 0.0)), 0.0)
            for h in heads:
                decay[ck, h] = dall[:, lanes[h]]
    else:
        decay = {}
        for ck, h in pairs:
            gcol = gc[ck][:, h * HEAD_DIM:h * HEAD_DIM + c]
            grow = sel_dot(ones_cc, gcol * eye)
            decay[ck, h] = jnp.where(incl, jnp.exp(jnp.where(incl, gcol - grow, 0.0)), 0.0)

    kk = {(ck, h): mm_nt(kb[ck][:, lanes[h]], kn[rows[ck], lanes[h]]) for ck, h in pairs}
    qk = {(ck, h): mm_nt(qn[rows[ck], lanes[h]], kn[rows[ck], lanes[h]]) for ck, h in pairs}
    kdec_t = {(ck, h): mm_nt(eye64, kdec[ck][:, lanes[h]]) for ck, h in pairs}
    lm = {p: jnp.where(strict, kk[p] * decay[p], 0.0) for p in pairs}
    intra = {p: jnp.where(incl, qk[p] * decay[p], 0.0) for p in pairs}
    tinv = {p: eye - lm[p] for p in pairs}
    pw = lm
    for _ in range(levels):
        pw = {p: dot_hl(pw[p], pw[p]) for p in pairs}
        tinv = {p: tinv[p] + dot_hl(tinv[p], pw[p]) for p in pairs}
    u = {(ck, h): dot_hl(tinv[ck, h], vb[ck][:, lanes[h]]) for ck, h in pairs}
    w = {(ck, h): dot_hl(tinv[ck, h], kbg[ck][:, lanes[h]]) for ck, h in pairs}

    for ck in range(nck):
        hst = [hs_ref[h] for h in heads]
        v_new = [u[ck, h] - mm(w[ck, h], hst[h]) for h in heads]
        o_st = [mm(qg[ck][:, lanes[h]], hst[h]) for h in heads]
        o = [o_st[h] + mm(intra[ck, h], v_new[h]) for h in heads]
        for h in heads:
            hs_ref[h] = hst[h] * hdec[ck][:, lanes[h]] + mm(kdec_t[ck, h], v_new[h])
        outs = []
        for h in heads:
            ms = jnp.mean(o[h] * o[h], axis=-1, keepdims=True)
            zh = z[rows[ck], lanes[h]]
            outs.append(o[h] * lax.rsqrt(ms + EPS) * an_ref[...] * (zh * _sigmoid(zh)))
        o_ref[0, rows[ck], :] = jnp.concatenate(outs, axis=1)

    @pl.when(t == nt - 1)
    def _():
        hout_ref[0] = hs_ref[...]


def _delta(qkv, z, misc, cprev8, h0, cw8, alog, dtb, an, consts, tc, chunk):
    b, t, _ = qkv.shape
    assert t % tc == 0 and tc % chunk == 0
    seg, selb, selg = consts
    bt = lambda width: pl.BlockSpec((1, tc, width), lambda i, j: (i, j, 0))
    const = lambda shape: pl.BlockSpec(shape, lambda i, j: tuple(0 for _ in shape))
    hspec = pl.BlockSpec((1, A_HEADS, HEAD_DIM, HEAD_DIM), lambda i, j: (i, 0, 0, 0))
    return pl.pallas_call(
        functools.partial(_delta_kernel, chunk=chunk),
        out_shape=(jax.ShapeDtypeStruct((b, t, A_WIDTH), F32),
                   jax.ShapeDtypeStruct((b, A_HEADS, HEAD_DIM, HEAD_DIM), F32)),
        grid=(b, t // tc),
        in_specs=[bt(3 * A_WIDTH), bt(A_WIDTH), bt(MISC_W),
                  pl.BlockSpec((1, CONV_HALO, 3 * A_WIDTH), lambda i, j: (i, 0, 0)),
                  hspec, const((CONV_HALO, 3 * A_WIDTH)), const((1, MISC_W)), const((1, MISC_W)),
                  const((1, HEAD_DIM)), const((A_WIDTH, A_WIDTH)), const((MISC_W, A_WIDTH)),
                  const((MISC_W, A_WIDTH))],
        out_specs=(bt(A_WIDTH), hspec),
        scratch_shapes=[pltpu.VMEM((tc + CONV_HALO, 3 * A_WIDTH), F32),
                        pltpu.VMEM((A_HEADS, HEAD_DIM, HEAD_DIM), F32)],
        compiler_params=_params("arbitrary", "arbitrary"),
        name="delta",
    )(qkv, z, misc, cprev8, h0, cw8, alog, dtb, an, seg, selb, selg)


PROMPT_QUERY_TILE = 2 * LANES_V7X
KEY_BLOCK = 512
KEY_STEP = 128
COUNT_ROWS = 4 * SUBLANES_V7X
BIT_GROUP_ROWS = 32 * SUBLANES_V7X


def _transpose32(words):
    a = list(words)
    j, mask = 16, 0x0000FFFF
    while j:
        k = 0
        while k < 32:
            t = (a[k] ^ (a[k + j] >> j)) & mask
            a[k] = a[k] ^ t
            a[k + j] = a[k + j] ^ (t << j)
            k = (k + j + 1) & ~j
        j >>= 1
        mask ^= (mask << j) & 0xFFFFFFFF
    return a


def _dsa_prompt_kernel(q_ref, qi_ref, miscq_ref, ki_ref, kt_ref, vt_ref, o_ref,
                       skey_ref, planes_ref, active_ref, qs_ref, qext_ref, bias_ref, s_ref, p_ref, m_ref, a_ref, l_ref, acc_ref,
                       *, k_sel, idx_bits, kb):
    i = pl.program_id(1)
    tq = q_ref.shape[1]
    ks = KEY_STEP
    nsub = kb // ks
    nkb = lax.div((i + 1) * tq + (kb - 1), kb)

    wt = miscq_ref[0].T[MISC_WI:MISC_WI + IDX_HEADS, :] * (IDX_DIM ** -0.5 * IDX_HEADS ** -0.5)
    qs_ref[...] = qi_ref[0].T.astype(BF16)
    qpos = i * tq + lax.broadcasted_iota(I32, (ks, tq), 1)
    krow = lax.broadcasted_iota(I32, (ks, tq), 0)

    def score_block(jb, carry):
        for u in range(nsub):
            rows = slice(u * ks, (u + 1) * ks)
            kit = ki_ref[0, jb, rows, :]
            ds = [jnp.dot(kit, qs_ref[h * IDX_DIM:(h + 1) * IDX_DIM, :], preferred_element_type=F32)
                  for h in range(IDX_HEADS)]
            s = jnp.maximum(ds[0], 0.0) * wt[0:1, :]
            for h in range(1, IDX_HEADS):
                s = s + jnp.maximum(ds[h], 0.0) * wt[h:h + 1, :]
            skey_ref[jb, rows, :] = _score_key(s)
        return carry

    lax.fori_loop(0, nkb, score_block, 0)
    last = nkb - 1
    for u in range(nsub):
        rows = slice(u * ks, (u + 1) * ks)
        adm = last * kb + u * ks + krow <= qpos
        skey_ref[last, rows, :] = jnp.where(adm, skey_ref[last, rows, :], NEG_INF_KEY)

    @pl.when((i == 0) & (pl.program_id(0) == 0))
    def _():
        planes_ref[...] = jnp.zeros(planes_ref.shape, I32)

    gpb = kb // BIT_GROUP_ROWS
    n_groups = planes_ref.shape[1]

    def plane_block(jb, carry):
        for gsub in range(gpb):
            for c0 in range(0, tq, LANES_V7X):
                cols = slice(c0, c0 + LANES_V7X)
                words = [skey_ref[jb, gsub * BIT_GROUP_ROWS + SUBLANES_V7X * k:
                                  gsub * BIT_GROUP_ROWS + SUBLANES_V7X * (k + 1), cols] for k in range(32)]
                for w, word in enumerate(_transpose32(words)):
                    planes_ref[31 - w, jb * gpb + gsub, :, cols] = word
        return carry

    lax.fori_loop(0, nkb, plane_block, 0)

    ng = nkb * gpb
    for g in range(n_groups):
        active_ref[g] = jnp.zeros((SUBLANES_V7X, tq), I32) + jnp.where(g < ng, -1, 0)

    def plane(bit, g):
        return ~planes_ref[bit, g] if bit == 31 else planes_ref[bit, g]

    above = jnp.zeros((1, tq), I32)
    m_bits = jnp.zeros((1, tq), I32)
    keep = None
    for bit in range(31, -2, -1):
        parts = [jnp.zeros((SUBLANES_V7X, tq), I32) for _ in range(4)]
        for g in range(n_groups):
            act = active_ref[g]
            if keep is not None:
                act = act & ~(plane(bit + 1, g) ^ keep)
                active_ref[g] = act
            hit = act & plane(bit, g) if bit >= 0 else act
            parts[g % 4] = parts[g % 4] + lax.population_count(hit)
        cnt = jnp.sum((parts[0] + parts[1]) + (parts[2] + parts[3]), axis=0, keepdims=True)
        if bit < 0:
            n_ge = above + cnt
            break
        take = above + cnt >= k_sel
        above = jnp.where(take, above, above + cnt)
        m_bits = m_bits | jnp.where(take, jnp.int32(INT_MIN if bit == 31 else 1 << bit), 0)
        keep = jnp.broadcast_to(jnp.where(take, -1, 0), (SUBLANES_V7X, tq))
    m = m_bits ^ INT_MIN

    def first_reaching(counts, target):
        cum = jnp.zeros((1, tq), I32)
        idx = jnp.zeros((1, tq), I32)
        before = jnp.zeros((1, tq), I32)
        found = jnp.zeros((1, tq), I32)
        for n, c in enumerate(counts):
            new = cum + c
            hit = jnp.where(new >= target, 1 - found, 0)
            idx = jnp.where(hit > 0, n, idx)
            before = jnp.where(hit > 0, cum, before)
            found = found + hit
            cum = new
        return idx, before, found

    def col_sum(x):
        return jnp.sum(x, axis=0, keepdims=True)

    need = k_sel - above
    acts = [active_ref[g] for g in range(n_groups)]
    g_idx, g_before, g_found = first_reaching([col_sum(lax.population_count(a)) for a in acts], need)
    word = jnp.zeros((SUBLANES_V7X, tq), I32)
    for g, a in enumerate(acts):
        word = jnp.where(g_idx == g, a, word)
    need_k = need - g_before
    rowbits = [(word >> (31 - k)) & 1 for k in range(32)]
    k_idx, k_before, _ = first_reaching([col_sum(rb) for rb in rowbits], need_k)
    rowsel = jnp.zeros((SUBLANES_V7X, tq), I32)
    for k, rb in enumerate(rowbits):
        rowsel = jnp.where(k_idx == k, rb, rowsel)
    need_s = need_k - k_before
    sub = lax.broadcasted_iota(I32, (SUBLANES_V7X, tq), 0)
    s_idx, _, _ = first_reaching([col_sum(jnp.where(sub == s, rowsel, 0)) for s in range(SUBLANES_V7X)], need_s)
    jcut = jnp.where(g_found > 0, g_idx * BIT_GROUP_ROWS + k_idx * SUBLANES_V7X + s_idx, BIG_IDX)

    q_t = (q_ref[0] * (HEAD_DIM ** -0.5 * math.log2(math.e))).T
    pair_row = lax.broadcasted_iota(I32, (LANES_V7X, tq), 0) // HEAD_DIM
    for h in range(C_HEADS):
        qg = q_t[(h // 2) * LANES_V7X:(h // 2 + 1) * LANES_V7X, :]
        qext_ref[h] = jnp.where(pair_row == h % 2, qg, 0.0).astype(BF16)
    m_ref[...] = jnp.full(m_ref.shape, MASKED, F32)
    l_ref[...] = jnp.zeros(l_ref.shape, F32)
    acc_ref[...] = jnp.zeros(acc_ref.shape, F32)
    sgrp = ks // SUBLANES_V7X
    dyn0 = pl.multiple_of(lax.shift_right_logical(i, 24) * ks, ks)

    def att_block(jb, carry):
        for u in range(nsub):
            rows = slice(u * ks, (u + 1) * ks)
            kidx = jb * kb + u * ks + krow
            sel = jnp.where(kidx <= qpos, _cut_rank(skey_ref[jb, rows, :], m, kidx), BIG_IDX) <= jcut
            bias_ref[rows, :] = jnp.where(sel, 0.0, MASKED)
        tops = [None] * C_HEADS
        for u in range(nsub):
            rows = slice(u * ks, (u + 1) * ks)
            for h in range(C_HEADS):
                g = h // 2
                kt = kt_ref[0, jb, rows, g * LANES_V7X:(g + 1) * LANES_V7X]
                s = bias_ref[rows, :] + jnp.dot(kt, qext_ref[h], preferred_element_type=F32)
                s_ref[h, rows, :] = s
                top = jnp.max(s.reshape(sgrp, SUBLANES_V7X, tq), axis=0)
                tops[h] = top if u == 0 else jnp.maximum(tops[h], top)
        for h in range(C_HEADS):
            m_old = m_ref[h]
            m_new = jnp.maximum(m_old, jnp.max(tops[h], axis=0, keepdims=True))
            m_ref[h] = m_new
            a_ref[h] = jnp.exp2(m_old - m_new)
        for h in range(C_HEADS):
            m_new = m_ref[h]
            part = a_ref[h] * l_ref[h]
            for u in range(nsub):
                rows = slice(u * ks, (u + 1) * ks)
                s = s_ref[h, pl.ds(dyn0 + u * ks, ks), :]
                p = jnp.exp2(s.reshape(sgrp, SUBLANES_V7X, tq) - m_new[None])
                part = part + jnp.sum(p, axis=0)
                p_ref[h, rows, :] = p.reshape(ks, tq).astype(BF16)
            l_ref[h] = part
        for h in range(C_HEADS):
            vt = vt_ref[0, jb, h * HEAD_DIM:(h + 1) * HEAD_DIM, :]
            p = p_ref[h, pl.ds(dyn0, kb), :]
            acc_ref[h] = a_ref[h][0:1, :] * acc_ref[h] + jnp.dot(vt, p, preferred_element_type=F32)
        return carry

    lax.fori_loop(0, nkb, att_block, 0)
    o_t = jnp.concatenate([acc_ref[h] / jnp.sum(l_ref[h], axis=0, keepdims=True) for h in range(C_HEADS)],
                          axis=0)
    o_ref[0] = o_t.T


def _dsa_prompt(q, k, v, qi, misc, tq):
    b, t, _ = q.shape
    kb = min(KEY_BLOCK, t)
    assert t % tq == 0 and tq % LANES_V7X == 0 and t % kb == 0 and kb % KEY_STEP == 0
    nb = t // kb
    k_sel = min(TOPK_MAX, t // 4)
    idx_bits = max(1, (t - 1).bit_length())
    ki = misc[:, :, MISC_KI:MISC_KI + IDX_DIM].astype(BF16).reshape(b, nb, kb, IDX_DIM)
    kt = k.reshape(b, nb, kb, C_WIDTH).astype(BF16)
    vt = jnp.swapaxes(v.reshape(b, nb, kb, C_WIDTH), 2, 3).astype(BF16)
    qblk = lambda width: pl.BlockSpec((1, tq, width), lambda i, j: (i, j, 0))
    seq = lambda d2, d3: pl.BlockSpec((1, nb, d2, d3), lambda i, j: (i, 0, 0, 0))
    return pl.pallas_call(
        functools.partial(_dsa_prompt_kernel, k_sel=k_sel, idx_bits=idx_bits, kb=kb),
        out_shape=jax.ShapeDtypeStruct((b, t, C_WIDTH), F32),
        grid=(b, t // tq),
        in_specs=[qblk(C_WIDTH), qblk(QI_WIDTH), qblk(MISC_W),
                  seq(kb, IDX_DIM), seq(kb, C_WIDTH), seq(C_WIDTH, kb)],
        out_specs=qblk(C_WIDTH),
        scratch_shapes=[pltpu.VMEM((nb, kb, tq), I32),
                        pltpu.VMEM((32, t // BIT_GROUP_ROWS, SUBLANES_V7X, tq), I32),
                        pltpu.VMEM((t // BIT_GROUP_ROWS, SUBLANES_V7X, tq), I32),
                        pltpu.VMEM((QI_WIDTH, tq), BF16),
                        pltpu.VMEM((C_HEADS, LANES_V7X, tq), BF16),
                        pltpu.VMEM((kb, tq), F32),
                        pltpu.VMEM((C_HEADS, kb, tq), F32),
                        pltpu.VMEM((C_HEADS, kb, tq), BF16),
                        pltpu.VMEM((C_HEADS, SUBLANES_V7X, tq), F32),
                        pltpu.VMEM((C_HEADS, SUBLANES_V7X, tq), F32),
                        pltpu.VMEM((C_HEADS, SUBLANES_V7X, tq), F32),
                        pltpu.VMEM((C_HEADS, HEAD_DIM, tq), F32)],
        compiler_params=_params("arbitrary", "arbitrary"),
        name="dsa_prompt",
    )(q, qi, misc, ki, kt, vt)


SAMPLE_PAGES_PER_STEP = 8


def _dsa_sample_index_kernel(pt_ref, *refs, k_sel, n_steps, pps, idx_bits):
    kidx_refs = refs[:pps]
    qi_ref, misc_ref, skp_ref, skn_ref, m_ref, jc_ref, sk_ref = refs[pps:]
    j = pl.program_id(1)
    t = qi_ref.shape[1]
    misc = misc_ref[0]
    qi = qi_ref[0]
    qs = jnp.concatenate([qi[:, h * IDX_DIM:(h + 1) * IDX_DIM] for h in range(IDX_HEADS)],
                         axis=0).astype(BF16)
    ws = jnp.concatenate([misc[:, MISC_WI + h:MISC_WI + h + 1] for h in range(IDX_HEADS)], axis=0)
    ws = ws * (IDX_DIM ** -0.5 * IDX_HEADS ** -0.5)
    lane = lax.broadcasted_iota(I32, (t, PAGE_SIZE), 1)
    row = lax.broadcasted_iota(I32, (t, PAGE_SIZE), 0)

    def keys_of(ki_tiles):
        ds = [lax.dot_general(qs, kt.astype(BF16), NT_DIMS, preferred_element_type=F32) for kt in ki_tiles]
        out = []
        for d in ds:
            sw = jnp.maximum(d, 0.0) * ws
            s = sw[0:t]
            for h in range(1, IDX_HEADS):
                s = s + sw[h * t:(h + 1) * t]
            out.append(_score_key(s))
        return out

    @pl.when(j < n_steps)
    def _():
        for r, key in enumerate(keys_of([kidx_refs[r][0, 0] for r in range(pps)])):
            sk_ref[j * pps + r] = key
            skp_ref[0, :, r * PAGE_SIZE:(r + 1) * PAGE_SIZE] = key

    @pl.when(j == n_steps)
    def _():
        ki_new = misc[:, MISC_KI:MISC_KI + IDX_DIM]
        ki_pad = jnp.concatenate([ki_new, jnp.zeros((PAGE_SIZE - t, IDX_DIM), F32)], axis=0)
        key = jnp.where(lane <= row, keys_of([ki_pad])[0], NEG_INF_KEY)
        sk_ref[n_steps * pps] = key
        skn_ref[0] = key
        allk = sk_ref[...]
        idx = (lax.broadcasted_iota(I32, allk.shape, 0) * PAGE_SIZE
               + lax.broadcasted_iota(I32, allk.shape, 2))

        def count(mask):
            return jnp.sum(jnp.sum(mask.astype(I32), axis=0), axis=1, keepdims=True)

        m = jnp.full((t, 1), INT_MIN, I32)
        for bit in range(31, -1, -1):
            cand = (m ^ INT_MIN) if bit == 31 else (m | (1 << bit))
            m = jnp.where(count(allk >= cand[None]) >= k_sel, cand, m)
        need = k_sel - count(allk >= (m + 1)[None])
        tied_idx = jnp.where(allk == m[None], idx, BIG_IDX)
        jcut = jnp.zeros((t, 1), I32)
        for bit in range(idx_bits - 1, -1, -1):
            cand = jcut | (1 << bit)
            below = count(tied_idx < cand[None])
            jcut = jnp.where(below < need, cand, jcut)
        m_ref[0] = jnp.broadcast_to(m, (t, LANES_V7X))
        jc_ref[0] = jnp.broadcast_to(jcut, (t, LANES_V7X))


def _dsa_sample_attend_kernel(pt_ref, *refs, n_steps, pps):
    kc_refs, vc_refs = refs[:pps], refs[pps:2 * pps]
    (skp_ref, skn_ref, m_ref, jc_ref, q_ref, kn_ref, vn_ref,
     o_ref, qbd_ref, ms_ref, ls_ref, acc_ref) = refs[2 * pps:]
    j = pl.program_id(1)
    t = q_ref.shape[1]

    @pl.when(j == 0)
    def _():
        q = q_ref[0] * (HEAD_DIM ** -0.5)
        head = lax.broadcasted_iota(I32, q.shape, 1) // HEAD_DIM
        qbd_ref[...] = jnp.concatenate([jnp.where(head == h, q, 0.0) for h in range(C_HEADS)], axis=0)
        ms_ref[...] = jnp.full(ms_ref.shape, MASKED, F32)
        ls_ref[...] = jnp.zeros(ls_ref.shape, F32)
        acc_ref[...] = jnp.zeros(acc_ref.shape, F32)

    m = m_ref[0]
    jcut6 = jnp.concatenate([jc_ref[0]] * C_HEADS, axis=0)
    lane = lax.broadcasted_iota(I32, m.shape, 1)
    row = lax.broadcasted_iota(I32, m.shape, 0)

    def step(keys, first_idx, last_adm, kps, vps):
        qbd = qbd_ref[...].astype(BF16)
        ss = [lax.dot_general(qbd, kp.astype(BF16), NT_DIMS, preferred_element_type=F32) for kp in kps]
        masked = []
        for r, s in enumerate(ss):
            rank = jnp.where(lane <= last_adm, _cut_rank(keys[r], m, first_idx + r * PAGE_SIZE + lane), BIG_IDX)
            sel = jnp.concatenate([rank] * C_HEADS, axis=0) <= jcut6
            masked.append(jnp.where(sel, s, MASKED))
        top = functools.reduce(jnp.maximum, masked)
        mprev = ms_ref[:, 0:1]
        mn = jnp.maximum(mprev, jnp.max(top, axis=1, keepdims=True))
        alpha = jnp.exp(mprev - mn)
        ps = [jnp.exp(s - mn) for s in masked]
        psum = functools.reduce(lambda a, b: a + b, ps)
        pv = [jnp.dot(p.astype(BF16), vp.astype(BF16), preferred_element_type=F32) for p, vp in zip(ps, vps)]
        ls_ref[...] = jnp.broadcast_to(alpha * ls_ref[:, 0:1] + jnp.sum(psum, axis=1, keepdims=True),
                                       ls_ref.shape)
        acc_ref[...] = alpha * acc_ref[...] + functools.reduce(lambda a, b: a + b, pv)
        ms_ref[...] = jnp.broadcast_to(mn, ms_ref.shape)

    @pl.when(j < n_steps)
    def _():
        keys = [skp_ref[0, :, r * PAGE_SIZE:(r + 1) * PAGE_SIZE] for r in range(pps)]
        step(keys, j * (pps * PAGE_SIZE), PAGE_SIZE,
             [kc_refs[r][0, 0] for r in range(pps)], [vc_refs[r][0, 0] for r in range(pps)])

    @pl.when(j == n_steps)
    def _():
        pad = jnp.zeros((PAGE_SIZE - t, C_WIDTH), F32)
        step([skn_ref[0]], n_steps * pps * PAGE_SIZE, row,
             [jnp.concatenate([kn_ref[0], pad], axis=0)], [jnp.concatenate([vn_ref[0], pad], axis=0)])
        o_all = acc_ref[...] / ls_ref[:, 0:1]
        head = lax.broadcasted_iota(I32, (t, C_WIDTH), 1) // HEAD_DIM
        o = jnp.zeros((t, C_WIDTH), F32)
        for h in range(C_HEADS):
            o = o + jnp.where(head == h, o_all[h * t:(h + 1) * t], 0.0)
        o_ref[0] = o


def _dsa_sample(cache_k, cache_v, cache_kidx, layer, page_table, q, k, v, qi, misc):
    b, t, _ = q.shape
    n_pages = page_table.shape[1]
    past = n_pages * PAGE_SIZE
    k_sel = min(TOPK_MAX, (past + t) // 4)
    idx_bits = max(1, (past + PAGE_SIZE - 1).bit_length())
    pps = math.gcd(n_pages, SAMPLE_PAGES_PER_STEP)
    n_steps = n_pages // pps

    def page(r):
        return lambda i, j, pt: (layer, pt[i, jnp.minimum(j, n_steps - 1) * pps + r], 0, 0)

    pages = lambda wd: [pl.BlockSpec((1, 1, PAGE_SIZE, wd), page(r)) for r in range(pps)]
    per_b = lambda wd: pl.BlockSpec((1, t, wd), lambda i, j, pt: (i, 0, 0))
    skp_spec = pl.BlockSpec((1, t, pps * PAGE_SIZE), lambda i, j, pt: (i, 0, jnp.minimum(j, n_steps - 1)))

    skp, skn, mth, jcut = pl.pallas_call(
        functools.partial(_dsa_sample_index_kernel, k_sel=k_sel, n_steps=n_steps, pps=pps, idx_bits=idx_bits),
        out_shape=(jax.ShapeDtypeStruct((b, t, past), I32),
                   jax.ShapeDtypeStruct((b, t, PAGE_SIZE), I32),
                   jax.ShapeDtypeStruct((b, t, LANES_V7X), I32),
                   jax.ShapeDtypeStruct((b, t, LANES_V7X), I32)),
        grid_spec=pltpu.PrefetchScalarGridSpec(
            num_scalar_prefetch=1,
            grid=(b, n_steps + 1),
            in_specs=pages(IDX_DIM) + [per_b(QI_WIDTH), per_b(MISC_W)],
            out_specs=(skp_spec, per_b(PAGE_SIZE), per_b(LANES_V7X), per_b(LANES_V7X)),
            scratch_shapes=[pltpu.VMEM((n_pages + 1, t, PAGE_SIZE), I32)]),
        compiler_params=_params("arbitrary", "arbitrary"),
        name="dsa_sample_index",
    )(page_table, *([cache_kidx] * pps), qi, misc)

    rows = C_HEADS * t
    return pl.pallas_call(
        functools.partial(_dsa_sample_attend_kernel, n_steps=n_steps, pps=pps),
        out_shape=jax.ShapeDtypeStruct((b, t, C_WIDTH), F32),
        grid_spec=pltpu.PrefetchScalarGridSpec(
            num_scalar_prefetch=1,
            grid=(b, n_steps + 1),
            in_specs=pages(C_WIDTH) + pages(C_WIDTH)
            + [skp_spec, per_b(PAGE_SIZE), per_b(LANES_V7X), per_b(LANES_V7X),
               per_b(C_WIDTH), per_b(C_WIDTH), per_b(C_WIDTH)],
            out_specs=per_b(C_WIDTH),
            scratch_shapes=[pltpu.VMEM((rows, C_WIDTH), F32), pltpu.VMEM((rows, LANES_V7X), F32),
                            pltpu.VMEM((rows, LANES_V7X), F32), pltpu.VMEM((rows, C_WIDTH), F32)]),
        compiler_params=_params("arbitrary", "arbitrary"),
        name="dsa_sample_attend",
    )(page_table, *([cache_k] * pps), *([cache_v] * pps), skp, skn, mth, jcut, q, k, v)


def _prep_w_in(w):
    o = np.cumsum([0, 3 * A_WIDTH, A_WIDTH, A_HEADS, A_HEADS, B_WIDTH, C_WIDTH, C_WIDTH, C_WIDTH,
                   QI_WIDTH, IDX_DIM, IDX_HEADS])
    part = lambda n: w[:, o[n]:o[n + 1]]
    qkv, z, bb, aa, u, q, k, v, qi, ki, wi = (part(n) for n in range(11))
    pad = jnp.zeros((w.shape[0], MISC_W - (IDX_DIM + IDX_HEADS + 2 * A_HEADS)), w.dtype)
    return jnp.concatenate([qkv, z, u, q, k, v, qi, ki, wi, bb, aa, pad], axis=1).astype(BF16)


def _rope_tabs(pos):
    inv = ROPE_THETA ** (-jnp.arange(ROT_HALF, dtype=F32) / ROT_HALF)
    ang = pos.astype(F32)[:, None] * inv[None, :]
    cos, sin = jnp.cos(ang), jnp.sin(ang)
    n = pos.shape[0]
    rest = HEAD_DIM - ROT_DIM
    ct = jnp.concatenate([cos, cos, jnp.ones((n, rest), F32)], axis=1)
    sa = jnp.concatenate([-sin, jnp.zeros((n, rest + ROT_HALF), F32)], axis=1)
    sb = jnp.concatenate([jnp.zeros((n, ROT_HALF), F32), sin, jnp.zeros((n, rest), F32)], axis=1)
    rep = LANES_V7X // HEAD_DIM
    return tuple(jnp.tile(a, (1, rep)) for a in (ct, sa, sb))


def _delta_consts():
    lane = np.arange(A_WIDTH)
    seg = (lane[:, None] // HEAD_DIM == lane[None, :] // HEAD_DIM).astype(np.float32)
    src = np.arange(MISC_W)
    selb = (src[:, None] == MISC_B + lane[None, :] // HEAD_DIM).astype(np.float32)
    selg = (src[:, None] == MISC_A + lane[None, :] // HEAD_DIM).astype(np.float32)
    return jnp.asarray(seg), jnp.asarray(selb), jnp.asarray(selg)


def _misc_lanes(vec, off):
    return jnp.zeros((1, MISC_W), F32).at[0, off:off + vec.shape[0]].set(vec.astype(F32))


def _block_diag(pw):
    n = pw.shape[0]
    out = jnp.zeros((n * POOL_GROUP, n * POOL_GROUP), pw.dtype)
    for gi in range(n):
        out = out.at[gi * POOL_GROUP:(gi + 1) * POOL_GROUP, gi * POOL_GROUP:(gi + 1) * POOL_GROUP].set(pw[gi])
    return out


def _layer(x2d, b, t, pos0, tabs, conv_prev, pool_prev, delta_prev, attend, wts, consts, tiles, final):
    (g1, w_in, conv_w, a_log, dt_bias, a_norm, pool_w, pool_scale, w_out, g2, w_gate, w_up, w_down, gf) = wts
    tm, tc, tt = tiles
    qkv, z, u, q, k, v, qi, misc = _in_proj(x2d, g1, w_in, tabs, t, tm)
    r3 = lambda a: a.reshape(b, t, a.shape[-1])
    qkv, z, u, q, k, v, qi, misc = (r3(a) for a in (qkv, z, u, q, k, v, qi, misc))

    cprev8 = jnp.concatenate([jnp.zeros((b, CONV_HALO - (CONV_W - 1), 3 * A_WIDTH), F32), conv_prev], axis=1)
    o_a, delta_new = _delta(qkv, z, misc, cprev8, delta_prev, conv_w, a_log, dt_bias, a_norm, consts,
                            tc, math.gcd(t, DELTA_CHUNK))
    prev16 = jnp.concatenate([jnp.zeros((b, POOL_HALO - POOL_STATE, B_WIDTH), F32), pool_prev], axis=1)
    o_b = _pool(u, prev16, pool_w, pool_scale, pos0, tt)
    o_c = attend(q, k, v, qi, misc)

    flat = lambda a: a.reshape(b * t, a.shape[-1])
    y = _out_ffn(x2d, flat(o_a), flat(o_b), flat(o_c), w_out, g2, w_gate, w_up, w_down, gf, tm, final)
    conv_new = jnp.concatenate([conv_prev, qkv], axis=1)[:, -(CONV_W - 1):]
    pool_new = jnp.concatenate([pool_prev, u], axis=1)[:, -POOL_STATE:]
    state = (k.reshape(b, t, C_HEADS, HEAD_DIM), v.reshape(b, t, C_HEADS, HEAD_DIM),
             misc[:, :, MISC_KI:MISC_KI + IDX_DIM], conv_new, pool_new, delta_new)
    return y, state


def kernel(x_prompt, x_sample, cache_k, cache_v, cache_kidx, state_conv, state_pool, state_delta,
           page_table, norm1, w_in, conv_w, a_log, dt_bias, a_norm, pool_w, pool_scale,
           w_out, norm2, w_gate, w_up, w_down, norm_f):
    depth = w_in.shape[0]
    bp, tp, _ = x_prompt.shape
    bs, ts, _ = x_sample.shape
    n_pages = page_table.shape[1]
    past = n_pages * PAGE_SIZE
    n_pool = cache_k.shape[1]

    tabs_p = _rope_tabs(jnp.arange(tp, dtype=I32))
    tabs_s = _rope_tabs(past + jnp.arange(ts, dtype=I32))
    consts = _delta_consts()
    ck = cache_k.reshape(depth, n_pool, PAGE_SIZE, C_WIDTH)
    cv = cache_v.reshape(depth, n_pool, PAGE_SIZE, C_WIDTH)
    gf = norm_f.reshape(1, D_MODEL)

    tiles_p = (min(512, bp * tp), min(128, tp), min(512, tp))
    tiles_s = (bs * ts, ts, ts)
    conv0 = jnp.zeros((bp, CONV_W - 1, 3 * A_WIDTH), F32)
    pool0 = jnp.zeros((bp, POOL_STATE, B_WIDTH), F32)
    delta0 = jnp.zeros((bp, A_HEADS, HEAD_DIM, HEAD_DIM), F32)

    hp = x_prompt.reshape(bp * tp, D_MODEL)
    hs = x_sample.reshape(bs * ts, D_MODEL)
    p_st, s_st = [], []
    for l in range(depth):
        cw8 = jnp.concatenate([conv_w[l], jnp.zeros((CONV_HALO - CONV_W, 3 * A_WIDTH), F32)], axis=0)
        wts = (norm1[l].reshape(1, D_MODEL), _prep_w_in(w_in[l]), cw8,
               _misc_lanes(a_log[l], MISC_A), _misc_lanes(dt_bias[l], MISC_A),
               a_norm[l].reshape(1, HEAD_DIM), _block_diag(pool_w[l]).astype(BF16),
               pool_scale[l].reshape(1, B_WIDTH), w_out[l].astype(BF16), norm2[l].reshape(1, D_MODEL),
               w_gate[l].astype(BF16), w_up[l].astype(BF16), w_down[l].astype(BF16), gf)
        final = l == depth - 1
        attend_p = functools.partial(_dsa_prompt, tq=math.gcd(tp, PROMPT_QUERY_TILE))
        hp, sp = _layer(hp, bp, tp, 0, tabs_p, conv0, pool0, delta0, attend_p, wts, consts, tiles_p, final)
        attend_s = functools.partial(_dsa_sample, ck, cv, cache_kidx, l, page_table)
        hs, ss = _layer(hs, bs, ts, past, tabs_s, state_conv[l], state_pool[l], state_delta[l],
                        attend_s, wts, consts, tiles_s, final)
        p_st.append(sp)
        s_st.append(ss)
    outs = [hp.reshape(bp, tp, D_MODEL), hs.reshape(bs, ts, D_MODEL)]
    for sts in (p_st, s_st):
        for n in range(6):
            outs.append(jnp.stack([s[n] for s in sts]))
    return tuple(outs)
```
